```python
import jax, jax.numpy as jnp
from jax import lax
import numpy as np

D_MODEL = 1024
BATCH = 8
SEQ = 8192
DEPTH = 2

RET_HEADS = 4
RET_HEAD_DIM = 128
RET_WIDTH = RET_HEADS * RET_HEAD_DIM
RET_CHUNK = 128
ROPE_BASE = 10000.0
HGRN_HEADS = 4
HGRN_HEAD_DIM = 128
HGRN_WIDTH = HGRN_HEADS * HGRN_HEAD_DIM
HGRN_CHUNK = 32
CONV_WIDTH = 512
CONV_KERNEL = 31
N_BRANCH = 3
D_FF = 2816
FFN_CONV_KERNEL = 3
ALPHA = (2.0 * DEPTH) ** 0.25
BETA = (8.0 * DEPTH) ** -0.25
LN_EPS = 1e-5
IN_SEGMENTS = (RET_WIDTH,) * 4 + (HGRN_WIDTH,) * 4 + (CONV_WIDTH,) * 2 + (N_BRANCH * D_MODEL,)
D_IN = sum(IN_SEGMENTS)
VALUE_SEGMENTS = (2, 6, 8)

kernel_name = "hybrid_retention_hgrn2_conformer_deepnorm"


def _layer_norm(x, g, b):
    xf = x.astype(jnp.float32)
    mu = xf.mean(-1, keepdims=True)
    var = jnp.square(xf - mu).mean(-1, keepdims=True)
    return ((xf - mu) * lax.rsqrt(var + LN_EPS)).astype(x.dtype) * g + b


def _causal_dwconv(u, w, b):
    k = w.shape[0]
    y = lax.conv_general_dilated(u, w[:, None, :].astype(u.dtype), window_strides=(1,),
                                 padding=[(k - 1, 0)], dimension_numbers=('NWC', 'WIO', 'NWC'),
                                 feature_group_count=u.shape[-1])
    return y + b


def _rope(t, positions):
    half = t.shape[-1] // 2
    inv = ROPE_BASE ** (-jnp.arange(half, dtype=jnp.float32) / half)
    ang = positions.astype(jnp.float32)[..., None] * inv
    cos, sin = jnp.cos(ang)[:, :, None, :], jnp.sin(ang)[:, :, None, :]
    t1, t2 = t[..., :half], t[..., half:]
    return jnp.concatenate([t1 * cos - t2 * sin, t1 * sin + t2 * cos], axis=-1)


def _to_chunks(t, c):
    b, s, h, d = t.shape
    return t.reshape(b, s // c, c, h, d).transpose(1, 0, 3, 2, 4)


def _from_chunks(t):
    n, b, h, c, d = t.shape
    return t.transpose(1, 0, 3, 2, 4).reshape(b, n * c, h, d)


def _retention(q, k, v):
    b, _, h, dk = q.shape
    dv = v.shape[-1]
    c = RET_CHUNK
    log_gamma = jnp.log1p(-jnp.exp2(-5.0 - jnp.arange(h, dtype=jnp.float32)))
    idx = jnp.arange(c, dtype=jnp.float32)
    rel = idx[:, None] - idx[None, :]
    decay_mask = jnp.where(rel >= 0, jnp.exp(log_gamma[:, None, None] * jnp.maximum(rel, 0.0)), 0.0)
    query_decay = jnp.exp(log_gamma[:, None] * (idx + 1.0))[:, :, None]
    key_decay = jnp.exp(log_gamma[:, None] * (c - 1.0 - idx))[:, :, None]
    chunk_decay = jnp.exp(log_gamma * c)[:, None, None]

    def step(state, inp):
        qc, kc, vc = inp
        scores = jnp.einsum('bhtd,bhsd->bhts', qc, kc) * decay_mask
        out = jnp.einsum('bhts,bhsv->bhtv', scores, vc)
        out = out + jnp.einsum('bhtd,bhdv->bhtv', qc, state) * query_decay
        state = chunk_decay * state + jnp.einsum('bhsd,bhsv->bhdv', kc * key_decay, vc)
        return state, out

    state0 = jnp.zeros((b, h, dk, dv), jnp.float32)
    _, out = lax.scan(step, state0, (_to_chunks(q, c), _to_chunks(k, c), _to_chunks(v, c)))
    return _from_chunks(out)


def _hgrn2(q, logf, k, v):
    b, _, h, dk = q.shape
    dv = v.shape[-1]
    c = HGRN_CHUNK
    causal = jnp.tril(jnp.ones((c, c), dtype=bool))[None, None, :, :, None]

    def step(state, inp):
        qc, gc, kc, vc = inp
        cum = jnp.cumsum(gc, axis=2)
        diff = cum[:, :, :, None, :] - cum[:, :, None, :, :]
        decay = jnp.exp(jnp.where(causal, diff, -jnp.inf))
        attn = jnp.einsum('bhtd,bhsd,bhtsd->bhts', qc, kc, decay)
        out = jnp.einsum('bhts,bhsv->bhtv', attn, vc)
        out = out + jnp.einsum('bhtd,bhdv->bhtv', qc * jnp.exp(cum), state)
        last = cum[:, :, -1:, :]
        state = jnp.exp(last)[:, :, 0, :, None] * state + \
            jnp.einsum('bhsd,bhsv->bhdv', kc * jnp.exp(last - cum), vc)
        return state, out

    state0 = jnp.zeros((b, h, dk, dv), jnp.float32)
    _, out = lax.scan(step, state0, (_to_chunks(q, c), _to_chunks(logf, c),
                                     _to_chunks(k, c), _to_chunks(v, c)))
    return _from_chunks(out)


def _token_mixer(h, positions, lb, w_in, ret_norm_g, hgrn_norm_g, conv_w, conv_b,
                 conv_ln_g, conv_ln_b, w_branch, w_out):
    bsz, s, _ = h.shape
    f32 = jnp.float32
    p = h @ w_in
    bounds = np.cumsum(IN_SEGMENTS)[:-1].tolist()
    rq, rk, rv, rg, gq, gf, gi, gg, ca, cb, gates = jnp.split(p, bounds, axis=-1)

    def heads(t, n):
        return t.reshape(bsz, s, n, -1).astype(f32)

    q = _rope(heads(rq, RET_HEADS), positions)
    k = _rope(heads(rk, RET_HEADS), positions) * (RET_HEAD_DIM ** -0.5)
    o = _retention(q, k, heads(rv, RET_HEADS))
    mu = o.mean(-1, keepdims=True)
    o = (o - mu) * lax.rsqrt(jnp.square(o - mu).mean(-1, keepdims=True) + LN_EPS)
    u_a = o.reshape(bsz, s, RET_WIDTH).astype(h.dtype) * ret_norm_g * jax.nn.silu(rg)

    lbh = lb.reshape(HGRN_HEADS, HGRN_HEAD_DIM)
    logf = jnp.logaddexp(jnp.log(lbh), jnp.log1p(-lbh) + jax.nn.log_sigmoid(heads(gf, HGRN_HEADS)))
    o = _hgrn2(jax.nn.silu(heads(gq, HGRN_HEADS)), logf, -jnp.expm1(logf), heads(gi, HGRN_HEADS))
    o = o * lax.rsqrt(jnp.square(o).mean(-1, keepdims=True) + LN_EPS)
    u_b = o.reshape(bsz, s, HGRN_WIDTH).astype(h.dtype) * hgrn_norm_g * jax.nn.silu(gg)

    u = _causal_dwconv(ca * jax.nn.sigmoid(cb), conv_w, conv_b)
    u_c = jax.nn.silu(_layer_norm(u, conv_ln_g, conv_ln_b))

    gates = jax.nn.sigmoid(gates.reshape(bsz, s, N_BRANCH, D_MODEL))
    y = gates[:, :, 0] * (u_a @ w_branch[0])
    y = y + gates[:, :, 1] * (u_b @ w_branch[1])
    y = y + gates[:, :, 2] * (u_c @ w_branch[2])
    return y @ w_out


def _conv_ffn(h, w_up, conv_w, conv_b, w_down):
    p = _causal_dwconv(h @ w_up, conv_w, conv_b)
    a, v = jnp.split(p, 2, axis=-1)
    return (jax.nn.silu(a) * v) @ w_down


def setup_inputs(seed: int = 0) -> dict:
    key = jax.random.key(seed)
    ks = jax.random.split(key, 24)
    f32 = jnp.float32
    nrm = lambda k, shape, s: jax.random.normal(k, shape, f32) * s
    col_scale = jnp.concatenate([jnp.full((n,), BETA if i in VALUE_SEGMENTS else 1.0, f32)
                                 for i, n in enumerate(IN_SEGMENTS)])
    positions = (jnp.arange(SEQ, dtype=jnp.int32)[None, :]
                 + jax.random.randint(ks[2], (BATCH, 1), 0, SEQ, dtype=jnp.int32))
    return {
        "x": nrm(ks[0], (BATCH, SEQ, D_MODEL), 1.0),
        "c": nrm(ks[1], (BATCH, D_MODEL), 1.0),
        "positions": positions,
        "w_ada": nrm(ks[3], (DEPTH, D_MODEL, 6 * D_MODEL), 0.5 * D_MODEL ** -0.5),
        "b_ada": nrm(ks[4], (DEPTH, 6 * D_MODEL), 0.01),
        "w_in": nrm(ks[5], (DEPTH, D_MODEL, D_IN), D_MODEL ** -0.5) * col_scale,
        "ret_norm_g": 1.0 + nrm(ks[6], (DEPTH, RET_WIDTH), 0.01),
        "hgrn_lb_logits": nrm(ks[7], (DEPTH, HGRN_WIDTH), 1.0),
        "hgrn_norm_g": 1.0 + nrm(ks[8], (DEPTH, HGRN_WIDTH), 0.01),
        "conv_w": nrm(ks[9], (DEPTH, CONV_KERNEL, CONV_WIDTH), CONV_KERNEL ** -0.5),
        "conv_b": nrm(ks[10], (DEPTH, CONV_WIDTH), 0.01),
        "conv_ln_g": 1.0 + nrm(ks[11], (DEPTH, CONV_WIDTH), 0.01),
        "conv_ln_b": nrm(ks[12], (DEPTH, CONV_WIDTH), 0.01),
        "w_branch": nrm(ks[13], (DEPTH, N_BRANCH, RET_WIDTH, D_MODEL), BETA * RET_WIDTH ** -0.5),
        "w_out": nrm(ks[14], (DEPTH, D_MODEL, D_MODEL), BETA * D_MODEL ** -0.5),
        "ln1_g": 1.0 + nrm(ks[15], (DEPTH, D_MODEL), 0.01),
        "ln1_b": nrm(ks[16], (DEPTH, D_MODEL), 0.01),
        "ffn_w_up": nrm(ks[17], (DEPTH, D_MODEL, 2 * D_FF), BETA * D_MODEL ** -0.5),
        "ffn_conv_w": nrm(ks[18], (DEPTH, FFN_CONV_KERNEL, 2 * D_FF), FFN_CONV_KERNEL ** -0.5),
        "ffn_conv_b": nrm(ks[19], (DEPTH, 2 * D_FF), 0.01),
        "ffn_w_down": nrm(ks[20], (DEPTH, D_FF, D_MODEL), BETA * D_FF ** -0.5),
        "ln2_g": 1.0 + nrm(ks[21], (DEPTH, D_MODEL), 0.01),
        "ln2_b": nrm(ks[22], (DEPTH, D_MODEL), 0.01),
    }


def reference(x, c, positions, w_ada, b_ada, w_in, ret_norm_g, hgrn_lb_logits, hgrn_norm_g,
              conv_w, conv_b, conv_ln_g, conv_ln_b, w_branch, w_out, ln1_g, ln1_b,
              ffn_w_up, ffn_conv_w, ffn_conv_b, ffn_w_down, ln2_g, ln2_b):
    lb_all = jnp.cumsum(jax.nn.softmax(hgrn_lb_logits.astype(jnp.float32), axis=0), axis=0)
    lb_all = lb_all - lb_all[0]
    cond = jax.nn.silu(c)
    for l in range(DEPTH):
        mod = cond @ w_ada[l] + b_ada[l]
        sh1, sc1, g1, sh2, sc2, g2 = [m[:, None, :] for m in jnp.split(mod, 6, axis=-1)]
        h = x * (1.0 + sc1) + sh1
        y = _token_mixer(h, positions, lb_all[l], w_in[l], ret_norm_g[l], hgrn_norm_g[l],
                         conv_w[l], conv_b[l], conv_ln_g[l], conv_ln_b[l], w_branch[l], w_out[l])
        x = _layer_norm(ALPHA * x + g1 * y, ln1_g[l], ln1_b[l])
        h = x * (1.0 + sc2) + sh2
        y = _conv_ffn(h, ffn_w_up[l], ffn_conv_w[l], ffn_conv_b[l], ffn_w_down[l])
        x = _layer_norm(ALPHA * x + g2 * y, ln2_g[l], ln2_b[l])
    return x
```

```python
import functools

import numpy as np
import jax
import jax.numpy as jnp
from jax import lax
from jax.experimental import pallas as pl
from jax.experimental.pallas import tpu as pltpu

F32 = jnp.float32
BF16 = jnp.bfloat16

D_MODEL = 1024
N_HEADS = 4
HEAD_DIM = 128
WIDTH = N_HEADS * HEAD_DIM
RET_CHUNK = 128
ROPE_BASE = 10000.0
CONV_KERNEL = 31
N_BRANCH = 3
D_FF = 2816
FFN_CONV_KERNEL = 3
LN_EPS = 1e-5
N_MOD = 6

V7X_LANES = 128
V7X_SUBLANES = 8
V7X_BF16_ROWS = 16
V7X_VMEM_LIMIT = 56 * 1024 * 1024

HGRN_TILE = 256
HGRN_LEVELS = 8
RET_TILE = 512
CONV_TILE = 512
CONV_HALO = 32
MERGE_TILE = 512
FFN_TILE = 512
FFN_HALO = V7X_BF16_ROWS
FFN_CHUNK = 256
ADA_COLS = 1536


def _silu(x):
    return x * jax.nn.sigmoid(x)


def _layer_norm(x, g, b):
    mu = jnp.mean(x, axis=-1, keepdims=True)
    xc = x - mu
    var = jnp.mean(xc * xc, axis=-1, keepdims=True)
    return xc * lax.rsqrt(var + LN_EPS) * g + b


def _dot(a, b):
    return jnp.dot(a, b, preferred_element_type=F32)


def _dot_nt(a, b):
    return lax.dot_general(a, b, (((1,), (1,)), ((), ())), preferred_element_type=F32)


def _dot_tn(a, b):
    return lax.dot_general(a, b, (((0,), (0,)), ((), ())), preferred_element_type=F32)


def _split3(x):
    hi = x.astype(BF16)
    r1 = x - hi.astype(F32)
    mid = r1.astype(BF16)
    lo = (r1 - mid.astype(F32)).astype(BF16)
    return hi, mid, lo


def _modulate(x, mod_ref, row):
    shift = mod_ref[0, row:row + 1, :]
    scale = mod_ref[0, row + 1:row + 2, :]
    return x * (1.0 + scale) + shift


def _params(semantics):
    return pltpu.CompilerParams(dimension_semantics=semantics, vmem_limit_bytes=V7X_VMEM_LIMIT)


def _const_spec(shape):
    return pl.BlockSpec(shape, lambda *_: (0,) * len(shape), pipeline_mode=pl.Buffered(1))


def _ada_kernel(c_ref, w_ref, b_ref, o_ref):
    cond = _silu(c_ref[...]).astype(BF16)
    o_ref[0] = _dot(cond, w_ref[0].astype(BF16)) + b_ref[0]


def _ada_mod(c, w_ada, b_ada):
    depth, _, n = w_ada.shape
    bsz = c.shape[0]
    return pl.pallas_call(
        _ada_kernel,
        grid=(depth, n // ADA_COLS),
        in_specs=[
            pl.BlockSpec((bsz, D_MODEL), lambda l, j: (0, 0)),
            pl.BlockSpec((1, D_MODEL, ADA_COLS), lambda l, j: (l, 0, j)),
            pl.BlockSpec((1, 1, ADA_COLS), lambda l, j: (l, 0, j)),
        ],
        out_specs=pl.BlockSpec((1, bsz, ADA_COLS), lambda l, j: (l, 0, j)),
        out_shape=jax.ShapeDtypeStruct((depth, bsz, n), F32),
        compiler_params=_params(("parallel", "parallel")),
        name="ada_mod",
    )(c, w_ada, b_ada.reshape(depth, 1, n))


def _rope_kernel(pos_ref, inv_ref, sign_ref, cos_ref, sin_ref):
    ang = pos_ref[0] * inv_ref[...]
    cos_ref[0] = jnp.cos(ang)
    sin_ref[0] = jnp.sin(ang) * sign_ref[...]


def _rope_tables(positions):
    bsz, seq = positions.shape
    half = HEAD_DIM // 2
    inv = ROPE_BASE ** (-jnp.arange(half, dtype=F32) / half)
    inv = jnp.concatenate([inv, inv]).reshape(1, HEAD_DIM)
    sign = jnp.concatenate([-jnp.ones((half,), F32), jnp.ones((half,), F32)]).reshape(1, HEAD_DIM)
    pos = jnp.broadcast_to(positions.astype(F32)[..., None], (bsz, seq, HEAD_DIM))
    tile = RET_TILE
    spec = pl.BlockSpec((1, tile, HEAD_DIM), lambda b, i: (b, i, 0))
    return pl.pallas_call(
        _rope_kernel,
        grid=(bsz, seq // tile),
        in_specs=[spec, _const_spec((1, HEAD_DIM)), _const_spec((1, HEAD_DIM))],
        out_specs=[spec, spec],
        out_shape=[jax.ShapeDtypeStruct((bsz, seq, HEAD_DIM), F32)] * 2,
        compiler_params=_params(("parallel", "parallel")),
        name="rope_tables",
    )(pos, inv, sign)


def _retention_constants():
    c = RET_CHUNK
    log_gamma = np.log1p(-np.exp2(-5.0 - np.arange(N_HEADS, dtype=np.float64)))
    idx = np.arange(c, dtype=np.float64)
    rel = idx[:, None] - idx[None, :]
    mask = np.where(rel >= 0, np.exp(log_gamma[:, None, None] * np.maximum(rel, 0.0)), 0.0)
    qdec = np.exp(log_gamma[:, None] * (idx + 1.0))
    kdec = np.exp(log_gamma[:, None] * (c - 1.0 - idx))
    ones = np.ones((1, 1, HEAD_DIM))
    cdec = tuple(float(v) for v in np.exp(log_gamma * c))
    return (jnp.asarray(mask, F32), jnp.asarray(qdec[:, :, None] * ones, F32),
            jnp.asarray(kdec[:, :, None] * ones, F32), cdec)


def _ret_kernel(x_ref, mod_ref, cos_ref, sin_ref, w_ref, g_ref, dm_ref, qd_ref, kd_ref,
                o_ref, state_ref, *, chunk_decay):
    @pl.when(pl.program_id(1) == 0)
    def _():
        state_ref[...] = jnp.zeros_like(state_ref)

    tile = x_ref.shape[1]
    h = _modulate(x_ref[0], mod_ref, 0).astype(BF16)
    p = _dot(h, w_ref[...])
    cosf = cos_ref[0]
    sinf = sin_ref[0]
    scale = HEAD_DIM ** -0.5
    for hd in range(N_HEADS):
        lo = hd * HEAD_DIM
        q = p[:, lo:lo + HEAD_DIM]
        k = p[:, WIDTH + lo:WIDTH + lo + HEAD_DIM]
        v = p[:, 2 * WIDTH + lo:2 * WIDTH + lo + HEAD_DIM].astype(BF16)
        gate = _silu(p[:, 3 * WIDTH + lo:3 * WIDTH + lo + HEAD_DIM])
        q = q * cosf + pltpu.roll(q, HEAD_DIM // 2, 1) * sinf
        k = (k * cosf + pltpu.roll(k, HEAD_DIM // 2, 1) * sinf) * scale
        norm_g = g_ref[:, lo:lo + HEAD_DIM]
        state = state_ref[hd]
        for c in range(tile // RET_CHUNK):
            r0 = c * RET_CHUNK
            qc = q[r0:r0 + RET_CHUNK].astype(BF16)
            kf = k[r0:r0 + RET_CHUNK]
            kc = kf.astype(BF16)
            vc = v[r0:r0 + RET_CHUNK]
            scores = _dot_nt(qc, kc) * dm_ref[hd]
            out = _dot(scores.astype(BF16), vc) + _dot(qc, state.astype(BF16)) * qd_ref[hd]
            state = chunk_decay[hd] * state + _dot_tn((kf * kd_ref[hd]).astype(BF16), vc)
            mu = jnp.mean(out, axis=-1, keepdims=True)
            oc = out - mu
            var = jnp.mean(oc * oc, axis=-1, keepdims=True)
            u = oc * lax.rsqrt(var + LN_EPS) * norm_g * gate[r0:r0 + RET_CHUNK]
            o_ref[0, r0:r0 + RET_CHUNK, lo:lo + HEAD_DIM] = u.astype(o_ref.dtype)
        state_ref[hd] = state


def _retention_branch(x, mod, cos_t, sin_t, w, norm_g):
    bsz, seq, _ = x.shape
    tile = RET_TILE
    dmask, qdec, kdec, cdec = _retention_constants()
    row = lambda b, i: (b, i, 0)
    return pl.pallas_call(
        functools.partial(_ret_kernel, chunk_decay=cdec),
        grid=(bsz, seq // tile),
        in_specs=[
            pl.BlockSpec((1, tile, D_MODEL), row),
            pl.BlockSpec((1, N_MOD, D_MODEL), lambda b, i: (b, 0, 0)),
            pl.BlockSpec((1, tile, HEAD_DIM), row),
            pl.BlockSpec((1, tile, HEAD_DIM), row),
            _const_spec((D_MODEL, 4 * WIDTH)),
            _const_spec((1, WIDTH)),
            _const_spec((N_HEADS, RET_CHUNK, RET_CHUNK)),
            _const_spec((N_HEADS, RET_CHUNK, HEAD_DIM)),
            _const_spec((N_HEADS, RET_CHUNK, HEAD_DIM)),
        ],
        out_specs=pl.BlockSpec((1, tile, WIDTH), row),
        out_shape=jax.ShapeDtypeStruct((bsz, seq, WIDTH), BF16),
        scratch_shapes=[pltpu.VMEM((N_HEADS, HEAD_DIM, HEAD_DIM), F32)],
        compiler_params=_params(("parallel", "arbitrary")),
        name="retention_branch",
    )(x, mod, cos_t, sin_t, w, norm_g.reshape(1, WIDTH), dmask, qdec, kdec)


def _hgrn_constants():
    n = HGRN_TILE
    t = np.arange(n)
    tri = (t[:, None] >= t[None, :]).astype(np.float32)
    x = t[:, None] ^ t[None, :]
    level = np.where(t[:, None] > t[None, :], np.floor(np.log2(np.maximum(x, 1))), -1)
    return jnp.asarray(tri, BF16), jnp.asarray(level, jnp.int32)


def _hgrn_level_exponent(cum, logf, level):
    n, width = cum.shape
    group = 2 << level
    sub = V7X_SUBLANES
    if level == 0:
        odd = (lax.broadcasted_iota(jnp.int32, (n, width), 0) & 1) == 1
        return jnp.where(odd, logf, 0.0)
    if group >= sub:
        c3 = cum.reshape(n // group, group, width)
        pivot = c3[:, group // 2 - 1:group // 2, :]
        return (-jnp.abs(c3 - pivot)).reshape(n, width)
    c3 = cum.reshape(n // sub, sub, width)
    row = lax.broadcasted_iota(jnp.int32, c3.shape, 1)
    pivot = jnp.where(row < 4, c3[:, 1:2, :], c3[:, 5:6, :])
    return (-jnp.abs(c3 - pivot)).reshape(n, width)


def _hgrn_kernel(x_ref, mod_ref, w_ref, lbl_ref, g_ref, tri_ref, lvl_ref, o_ref, state_ref, *, layer):
    @pl.when(pl.program_id(1) == 0)
    def _():
        state_ref[...] = jnp.zeros_like(state_ref)

    logits = lbl_ref[...]
    e = jnp.exp(logits - jnp.max(logits, axis=0, keepdims=True))
    sm = e / jnp.sum(e, axis=0, keepdims=True)
    lb = jnp.zeros((1, WIDTH), F32)
    for j in range(1, layer + 1):
        lb = lb + sm[j:j + 1, :]
    log_lb = jnp.log(lb)
    log_1m_lb = jnp.log(1.0 - lb)

    h = _modulate(x_ref[0], mod_ref, 0).astype(BF16)
    p = _dot(h, w_ref[...])
    q = _silu(p[:, 0:WIDTH])
    gf = p[:, WIDTH:2 * WIDTH]
    v = p[:, 2 * WIDTH:3 * WIDTH].astype(BF16)
    gate = _silu(p[:, 3 * WIDTH:4 * WIDTH])
    log_sig = jnp.minimum(gf, 0.0) - jnp.log(1.0 + jnp.exp(-jnp.abs(gf)))
    b = log_1m_lb + log_sig
    logf = jnp.maximum(log_lb, b) + jnp.log(1.0 + jnp.exp(-jnp.abs(log_lb - b)))
    k = 1.0 - jnp.exp(logf)

    tri = tri_ref[...]
    f_hi, f_mid, f_lo = _split3(logf)
    cum = _dot(tri, f_hi) + _dot(tri, f_mid) + _dot(tri, f_lo)
    last = cum[HGRN_TILE - 1:HGRN_TILE, :]

    qe = (q * jnp.exp(cum)).astype(BF16)
    ke = (k * jnp.exp(last - cum)).astype(BF16)
    qk_diag = q * k
    lvl = lvl_ref[...]

    attn = [jnp.zeros((HGRN_TILE, HGRN_TILE), F32) for _ in range(N_HEADS)]
    for level in range(HGRN_LEVELS):
        ex = jnp.exp(_hgrn_level_exponent(cum, logf, level))
        qa = (q * ex).astype(BF16)
        kb = (k * ex).astype(BF16)
        sel = lvl == level
        for hd in range(N_HEADS):
            lo = hd * HEAD_DIM
            pr = _dot_nt(qa[:, lo:lo + HEAD_DIM], kb[:, lo:lo + HEAD_DIM])
            attn[hd] = jnp.where(sel, pr, attn[hd])

    for hd in range(N_HEADS):
        lo = hd * HEAD_DIM
        vh = v[:, lo:lo + HEAD_DIM]
        state_t = state_ref[hd]
        out = _dot(attn[hd].astype(BF16), vh)
        out = out + jnp.sum(qk_diag[:, lo:lo + HEAD_DIM], axis=-1, keepdims=True) * vh.astype(F32)
        out = out + _dot_nt(qe[:, lo:lo + HEAD_DIM], state_t.astype(BF16))
        state_ref[hd] = state_t * jnp.exp(last[:, lo:lo + HEAD_DIM]) + _dot_tn(vh, ke[:, lo:lo + HEAD_DIM])
        ms = jnp.mean(out * out, axis=-1, keepdims=True)
        u = out * lax.rsqrt(ms + LN_EPS) * g_ref[:, lo:lo + HEAD_DIM] * gate[:, lo:lo + HEAD_DIM]
        o_ref[0, :, lo:lo + HEAD_DIM] = u.astype(o_ref.dtype)


def _hgrn_branch(x, mod, w, lb_logits, norm_g, layer):
    bsz, seq, _ = x.shape
    tile = HGRN_TILE
    depth = lb_logits.shape[0]
    tri, lvl = _hgrn_constants()
    row = lambda b, i: (b, i, 0)
    return pl.pallas_call(
        functools.partial(_hgrn_kernel, layer=layer),
        grid=(bsz, seq // tile),
        in_specs=[
            pl.BlockSpec((1, tile, D_MODEL), row),
            pl.BlockSpec((1, N_MOD, D_MODEL), lambda b, i: (b, 0, 0)),
            _const_spec((D_MODEL, 4 * WIDTH)),
            _const_spec((depth, WIDTH)),
            _const_spec((1, WIDTH)),
            _const_spec((tile, tile)),
            _const_spec((tile, tile)),
        ],
        out_specs=pl.BlockSpec((1, tile, WIDTH), row),
        out_shape=jax.ShapeDtypeStruct((bsz, seq, WIDTH), BF16),
        scratch_shapes=[pltpu.VMEM((N_HEADS, HEAD_DIM, HEAD_DIM), F32)],
        compiler_params=_params(("parallel", "arbitrary")),
        name="hgrn_branch",
    )(x, mod, w, lb_logits.astype(F32), norm_g.reshape(1, WIDTH), tri, lvl)


def _conv_kernel(x_ref, mod_ref, w_ref, cw_ref, cb_ref, lg_ref, lb_ref, o_ref, buf_ref):
    tile = x_ref.shape[1]

    @pl.when(pl.program_id(1) == 0)
    def _():
        buf_ref[0:CONV_HALO, :] = jnp.zeros((CONV_HALO, WIDTH), F32)

    h = _modulate(x_ref[0], mod_ref, 0).astype(BF16)
    p = _dot(h, w_ref[...])
    buf_ref[CONV_HALO:CONV_HALO + tile, :] = p[:, 0:WIDTH] * jax.nn.sigmoid(p[:, WIDTH:2 * WIDTH])
    first = CONV_HALO - (CONV_KERNEL - 1)
    acc = jnp.zeros((tile, WIDTH), F32) + cb_ref[...]
    for j in range(CONV_KERNEL):
        acc = acc + buf_ref[first + j:first + j + tile, :] * cw_ref[j:j + 1, :]
    tail = buf_ref[tile:tile + CONV_HALO, :]
    buf_ref[0:CONV_HALO, :] = tail
    o_ref[0] = _silu(_layer_norm(acc, lg_ref[...], lb_ref[...])).astype(o_ref.dtype)


def _conv_branch(x, mod, w, conv_w, conv_b, ln_g, ln_b):
    bsz, seq, _ = x.shape
    tile = CONV_TILE
    row = lambda b, i: (b, i, 0)
    return pl.pallas_call(
        _conv_kernel,
        grid=(bsz, seq // tile),
        in_specs=[
            pl.BlockSpec((1, tile, D_MODEL), row),
            pl.BlockSpec((1, N_MOD, D_MODEL), lambda b, i: (b, 0, 0)),
            _const_spec((D_MODEL, 2 * WIDTH)),
            _const_spec((CONV_KERNEL, WIDTH)),
            _const_spec((1, WIDTH)),
            _const_spec((1, WIDTH)),
            _const_spec((1, WIDTH)),
        ],
        out_specs=pl.BlockSpec((1, tile, WIDTH), row),
        out_shape=jax.ShapeDtypeStruct((bsz, seq, WIDTH), BF16),
        scratch_shapes=[pltpu.VMEM((CONV_HALO + tile, WIDTH), F32)],
        compiler_params=_params(("parallel", "arbitrary")),
        name="conv_branch",
    )(x, mod, w, conv_w, conv_b.reshape(1, WIDTH), ln_g.reshape(1, WIDTH), ln_b.reshape(1, WIDTH))


def _merge_kernel(x_ref, mod_ref, ua_ref, ub_ref, uc_ref, wg_ref, wb_ref, wo_ref, lg_ref, lb_ref,
                  o_ref, *, alpha):
    x = x_ref[0]
    h = _modulate(x, mod_ref, 0).astype(BF16)
    y = None
    for i, u_ref in enumerate((ua_ref, ub_ref, uc_ref)):
        gate = jax.nn.sigmoid(_dot(h, wg_ref[:, i * D_MODEL:(i + 1) * D_MODEL]))
        term = gate * _dot(u_ref[0], wb_ref[i])
        y = term if y is None else y + term
    z = _dot(y.astype(BF16), wo_ref[...])
    res_gate = mod_ref[0, 2:3, :]
    o_ref[0] = _layer_norm(alpha * x + res_gate * z, lg_ref[...], lb_ref[...])


def _merge(x, mod, ua, ub, uc, w_gate, w_branch, w_out, ln_g, ln_b, alpha):
    bsz, seq, _ = x.shape
    tile = MERGE_TILE
    row = lambda b, i: (b, i, 0)
    u_spec = pl.BlockSpec((1, tile, WIDTH), row)
    return pl.pallas_call(
        functools.partial(_merge_kernel, alpha=alpha),
        grid=(bsz, seq // tile),
        in_specs=[
            pl.BlockSpec((1, tile, D_MODEL), row),
            pl.BlockSpec((1, N_MOD, D_MODEL), lambda b, i: (b, 0, 0)),
            u_spec, u_spec, u_spec,
            _const_spec((D_MODEL, N_BRANCH * D_MODEL)),
            _const_spec((N_BRANCH, WIDTH, D_MODEL)),
            _const_spec((D_MODEL, D_MODEL)),
            _const_spec((1, D_MODEL)),
            _const_spec((1, D_MODEL)),
        ],
        out_specs=pl.BlockSpec((1, tile, D_MODEL), row),
        out_shape=jax.ShapeDtypeStruct((bsz, seq, D_MODEL), F32),
        compiler_params=_params(("parallel", "parallel")),
        name="merge_ln",
    )(x, mod, ua, ub, uc, w_gate, w_branch, w_out, ln_g.reshape(1, D_MODEL), ln_b.reshape(1, D_MODEL))


def _ffn_kernel(x_ref, mod_ref, wu_ref, cw_ref, cb_ref, wd_ref, lg_ref, lb_ref,
                o_ref, hext_ref, pa_ref, pv_ref, acc_ref, *, alpha):
    tile = x_ref.shape[1]

    @pl.when(pl.program_id(1) == 0)
    def _():
        hext_ref[0:FFN_HALO, :] = jnp.zeros((FFN_HALO, D_MODEL), BF16)

    x = x_ref[0]
    hext_ref[FFN_HALO:FFN_HALO + tile, :] = _modulate(x, mod_ref, 3).astype(BF16)
    acc_ref[...] = jnp.zeros_like(acc_ref)
    first = FFN_HALO - (FFN_CONV_KERNEL - 1)

    def conv(p_ref, col):
        y = cb_ref[:, pl.ds(col, FFN_CHUNK)]
        for j in range(FFN_CONV_KERNEL):
            y = y + p_ref[first + j:first + j + tile, :] * cw_ref[j:j + 1, pl.ds(col, FFN_CHUNK)]
        return y

    def body(c, carry):
        col_a = pl.multiple_of(c * FFN_CHUNK, V7X_LANES)
        col_v = pl.multiple_of(D_FF + c * FFN_CHUNK, V7X_LANES)
        hext = hext_ref[...]
        pa_ref[...] = _dot(hext, wu_ref[:, pl.ds(col_a, FFN_CHUNK)])
        pv_ref[...] = _dot(hext, wu_ref[:, pl.ds(col_v, FFN_CHUNK)])
        act = _silu(conv(pa_ref, col_a)) * conv(pv_ref, col_v)
        acc_ref[...] += _dot(act.astype(BF16), wd_ref[pl.ds(col_a, FFN_CHUNK), :])
        return carry

    lax.fori_loop(0, D_FF // FFN_CHUNK, body, 0)
    hext_ref[0:FFN_HALO, :] = hext_ref[tile:tile + FFN_HALO, :]
    res_gate = mod_ref[0, 5:6, :]
    o_ref[0] = _layer_norm(alpha * x + res_gate * acc_ref[...], lg_ref[...], lb_ref[...])


def _conv_ffn(x, mod, w_up, conv_w, conv_b, w_down, ln_g, ln_b, alpha):
    bsz, seq, _ = x.shape
    tile = FFN_TILE
    row = lambda b, i: (b, i, 0)
    return pl.pallas_call(
        functools.partial(_ffn_kernel, alpha=alpha),
        grid=(bsz, seq // tile),
        in_specs=[
            pl.BlockSpec((1, tile, D_MODEL), row),
            pl.BlockSpec((1, N_MOD, D_MODEL), lambda b, i: (b, 0, 0)),
            _const_spec((D_MODEL, 2 * D_FF)),
            _const_spec((FFN_CONV_KERNEL, 2 * D_FF)),
            _const_spec((1, 2 * D_FF)),
            _const_spec((D_FF, D_MODEL)),
            _const_spec((1, D_MODEL)),
            _const_spec((1, D_MODEL)),
        ],
        out_specs=pl.BlockSpec((1, tile, D_MODEL), row),
        out_shape=jax.ShapeDtypeStruct((bsz, seq, D_MODEL), F32),
        scratch_shapes=[
            pltpu.VMEM((FFN_HALO + tile, D_MODEL), BF16),
            pltpu.VMEM((FFN_HALO + tile, FFN_CHUNK), F32),
            pltpu.VMEM((FFN_HALO + tile, FFN_CHUNK), F32),
            pltpu.VMEM((tile, D_MODEL), F32),
        ],
        compiler_params=_params(("parallel", "arbitrary")),
        name="conv_ffn_ln",
    )(x, mod, w_up, conv_w, conv_b.reshape(1, 2 * D_FF), w_down,
      ln_g.reshape(1, D_MODEL), ln_b.reshape(1, D_MODEL))


def kernel(x, c, positions, w_ada, b_ada, w_in, ret_norm_g, hgrn_lb_logits, hgrn_norm_g, conv_w, conv_b,
           conv_ln_g, conv_ln_b, w_branch, w_out, ln1_g, ln1_b, ffn_w_up, ffn_conv_w, ffn_conv_b,
           ffn_w_down, ln2_g, ln2_b):
    depth = w_ada.shape[0]
    bsz = x.shape[0]
    alpha = (2.0 * depth) ** 0.25
    mod_all = _ada_mod(c, w_ada, b_ada)
    cos_t, sin_t = _rope_tables(positions)
    ret_end = 4 * WIDTH
    hgrn_end = ret_end + 4 * WIDTH
    conv_end = hgrn_end + 2 * WIDTH
    for l in range(depth):
        mod = mod_all[l].reshape(bsz, N_MOD, D_MODEL)
        w_l = w_in[l].astype(BF16)
        ua = _retention_branch(x, mod, cos_t, sin_t, w_l[:, :ret_end], ret_norm_g[l])
        ub = _hgrn_branch(x, mod, w_l[:, ret_end:hgrn_end], hgrn_lb_logits, hgrn_norm_g[l], l)
        uc = _conv_branch(x, mod, w_l[:, hgrn_end:conv_end], conv_w[l], conv_b[l], conv_ln_g[l], conv_ln_b[l])
        x = _merge(x, mod, ua, ub, uc, w_l[:, conv_end:], w_branch[l].astype(BF16), w_out[l].astype(BF16),
                   ln1_g[l], ln1_b[l], alpha)
        x = _conv_ffn(x, mod, ffn_w_up[l].astype(BF16), ffn_conv_w[l], ffn_conv_b[l],
                      ffn_w_down[l].astype(BF16), ln2_g[l], ln2_b[l], alpha)
    return x
```

```python
import functools

import numpy as np
import jax
import jax.numpy as jnp
from jax import lax
from jax.experimental import pallas as pl
from jax.experimental.pallas import tpu as pltpu

F32 = jnp.float32
BF16 = jnp.bfloat16

D_MODEL = 1024
N_HEADS = 4
HEAD_DIM = 128
WIDTH = N_HEADS * HEAD_DIM
RET_CHUNK = 128
ROPE_BASE = 10000.0
CONV_KERNEL = 31
N_BRANCH = 3
D_FF = 2816
FFN_CONV_KERNEL = 3
LN_EPS = 1e-5
N_MOD = 6

V7X_LANES = 128
V7X_SUBLANES = 8
V7X_BF16_ROWS = 16
V7X_VMEM_LIMIT = 56 * 1024 * 1024

HGRN_TILE = 256
HGRN_LEVELS = 8
HGRN_BLOCK = HGRN_TILE // 2
LOG2_E = 1.4426950408889634
RET_TILE = 512
CONV_TILE = 512
CONV_HALO = 32
MERGE_TILE = 512
FFN_TILE = 512
FFN_HALO = V7X_BF16_ROWS
FFN_CHUNK = 256
ADA_COLS = 1536


def _sigmoid(x):
    return 0.5 * jnp.tanh(0.5 * x) + 0.5


def _silu(x):
    return x * _sigmoid(x)


def _layer_norm(x, g, b):
    mu = jnp.mean(x, axis=-1, keepdims=True)
    xc = x - mu
    var = jnp.mean(xc * xc, axis=-1, keepdims=True)
    return xc * lax.rsqrt(var + LN_EPS) * g + b


def _dot(a, b):
    return jnp.dot(a, b, preferred_element_type=F32)


def _dot_nt(a, b):
    return lax.dot_general(a, b, (((1,), (1,)), ((), ())), preferred_element_type=F32)


def _dot_tn(a, b):
    return lax.dot_general(a, b, (((0,), (0,)), ((), ())), preferred_element_type=F32)


def _split3(x):
    hi = x.astype(BF16)
    r1 = x - hi.astype(F32)
    mid = r1.astype(BF16)
    lo = (r1 - mid.astype(F32)).astype(BF16)
    return hi, mid, lo


def _modulate(x, mod_ref, row):
    shift = mod_ref[0, row:row + 1, :]
    scale = mod_ref[0, row + 1:row + 2, :]
    return x * (1.0 + scale) + shift


def _params(semantics):
    return pltpu.CompilerParams(dimension_semantics=semantics, vmem_limit_bytes=V7X_VMEM_LIMIT)


def _const_spec(shape):
    return pl.BlockSpec(shape, lambda *_: (0,) * len(shape), pipeline_mode=pl.Buffered(1))


def _ada_kernel(c_ref, w_ref, b_ref, o_ref):
    cond = _silu(c_ref[...]).astype(BF16)
    o_ref[0] = _dot(cond, w_ref[0].astype(BF16)) + b_ref[0]


def _ada_mod(c, w_ada, b_ada):
    depth, _, n = w_ada.shape
    bsz = c.shape[0]
    return pl.pallas_call(
        _ada_kernel,
        grid=(depth, n // ADA_COLS),
        in_specs=[
            pl.BlockSpec((bsz, D_MODEL), lambda l, j: (0, 0)),
            pl.BlockSpec((1, D_MODEL, ADA_COLS), lambda l, j: (l, 0, j)),
            pl.BlockSpec((1, 1, ADA_COLS), lambda l, j: (l, 0, j)),
        ],
        out_specs=pl.BlockSpec((1, bsz, ADA_COLS), lambda l, j: (l, 0, j)),
        out_shape=jax.ShapeDtypeStruct((depth, bsz, n), F32),
        compiler_params=_params(("parallel", "parallel")),
        name="ada_mod",
    )(c, w_ada, b_ada.reshape(depth, 1, n))


def _rope_kernel(pos_ref, inv_ref, sign_ref, cos_ref, sin_ref):
    ang = pos_ref[0] * inv_ref[...]
    cos_ref[0] = jnp.cos(ang)
    sin_ref[0] = jnp.sin(ang) * sign_ref[...]


def _rope_tables(positions):
    bsz, seq = positions.shape
    half = HEAD_DIM // 2
    inv = ROPE_BASE ** (-jnp.arange(half, dtype=F32) / half)
    inv = jnp.concatenate([inv, inv]).reshape(1, HEAD_DIM)
    sign = jnp.concatenate([-jnp.ones((half,), F32), jnp.ones((half,), F32)]).reshape(1, HEAD_DIM)
    pos = jnp.broadcast_to(positions.astype(F32)[..., None], (bsz, seq, HEAD_DIM))
    tile = RET_TILE
    spec = pl.BlockSpec((1, tile, HEAD_DIM), lambda b, i: (b, i, 0))
    return pl.pallas_call(
        _rope_kernel,
        grid=(bsz, seq // tile),
        in_specs=[spec, _const_spec((1, HEAD_DIM)), _const_spec((1, HEAD_DIM))],
        out_specs=[spec, spec],
        out_shape=[jax.ShapeDtypeStruct((bsz, seq, HEAD_DIM), F32)] * 2,
        compiler_params=_params(("parallel", "parallel")),
        name="rope_tables",
    )(pos, inv, sign)


def _retention_constants():
    c = RET_CHUNK
    log_gamma = np.log1p(-np.exp2(-5.0 - np.arange(N_HEADS, dtype=np.float64)))
    idx = np.arange(c, dtype=np.float64)
    rel = idx[:, None] - idx[None, :]
    mask = np.where(rel >= 0, np.exp(log_gamma[:, None, None] * np.maximum(rel, 0.0)), 0.0)
    qdec = np.exp(log_gamma[:, None] * (idx + 1.0))
    kdec = np.exp(log_gamma[:, None] * (c - 1.0 - idx))
    ones = np.ones((1, 1, HEAD_DIM))
    cdec = tuple(float(v) for v in np.exp(log_gamma * c))
    return (jnp.asarray(mask, F32), jnp.asarray(qdec[:, :, None] * ones, F32),
            jnp.asarray(kdec[:, :, None] * ones, F32), cdec)


def _ret_kernel(x_ref, mod_ref, cos_ref, sin_ref, w_ref, g_ref, dm_ref, qd_ref, kd_ref,
                o_ref, state_ref, *, chunk_decay):
    @pl.when(pl.program_id(1) == 0)
    def _():
        state_ref[...] = jnp.zeros_like(state_ref)

    tile = x_ref.shape[1]
    h = _modulate(x_ref[0], mod_ref, 0).astype(BF16)
    p = _dot(h, w_ref[...])
    cosf = cos_ref[0]
    sinf = sin_ref[0]
    scale = HEAD_DIM ** -0.5
    for hd in range(N_HEADS):
        lo = hd * HEAD_DIM
        q = p[:, lo:lo + HEAD_DIM]
        k = p[:, WIDTH + lo:WIDTH + lo + HEAD_DIM]
        v = p[:, 2 * WIDTH + lo:2 * WIDTH + lo + HEAD_DIM].astype(BF16)
        gate = _silu(p[:, 3 * WIDTH + lo:3 * WIDTH + lo + HEAD_DIM])
        q = q * cosf + pltpu.roll(q, HEAD_DIM // 2, 1) * sinf
        k = (k * cosf + pltpu.roll(k, HEAD_DIM // 2, 1) * sinf) * scale
        norm_g = g_ref[:, lo:lo + HEAD_DIM]
        state = state_ref[hd]
        for c in range(tile // RET_CHUNK):
            r0 = c * RET_CHUNK
            qc = q[r0:r0 + RET_CHUNK].astype(BF16)
            kf = k[r0:r0 + RET_CHUNK]
            kc = kf.astype(BF16)
            vc = v[r0:r0 + RET_CHUNK]
            scores = _dot_nt(qc, kc) * dm_ref[hd]
            out = _dot(scores.astype(BF16), vc) + _dot(qc, state.astype(BF16)) * qd_ref[hd]
            state = chunk_decay[hd] * state + _dot_tn((kf * kd_ref[hd]).astype(BF16), vc)
            mu = jnp.mean(out, axis=-1, keepdims=True)
            oc = out - mu
            var = jnp.mean(oc * oc, axis=-1, keepdims=True)
            u = oc * lax.rsqrt(var + LN_EPS) * norm_g * gate[r0:r0 + RET_CHUNK]
            o_ref[0, r0:r0 + RET_CHUNK, lo:lo + HEAD_DIM] = u.astype(o_ref.dtype)
        state_ref[hd] = state


def _retention_branch(x, mod, cos_t, sin_t, w, norm_g):
    bsz, seq, _ = x.shape
    tile = RET_TILE
    dmask, qdec, kdec, cdec = _retention_constants()
    row = lambda b, i: (b, i, 0)
    return pl.pallas_call(
        functools.partial(_ret_kernel, chunk_decay=cdec),
        grid=(bsz, seq // tile),
        in_specs=[
            pl.BlockSpec((1, tile, D_MODEL), row),
            pl.BlockSpec((1, N_MOD, D_MODEL), lambda b, i: (b, 0, 0)),
            pl.BlockSpec((1, tile, HEAD_DIM), row),
            pl.BlockSpec((1, tile, HEAD_DIM), row),
            _const_spec((D_MODEL, 4 * WIDTH)),
            _const_spec((1, WIDTH)),
            _const_spec((N_HEADS, RET_CHUNK, RET_CHUNK)),
            _const_spec((N_HEADS, RET_CHUNK, HEAD_DIM)),
            _const_spec((N_HEADS, RET_CHUNK, HEAD_DIM)),
        ],
        out_specs=pl.BlockSpec((1, tile, WIDTH), row),
        out_shape=jax.ShapeDtypeStruct((bsz, seq, WIDTH), BF16),
        scratch_shapes=[pltpu.VMEM((N_HEADS, HEAD_DIM, HEAD_DIM), F32)],
        compiler_params=_params(("parallel", "arbitrary")),
        name="retention_branch",
    )(x, mod, cos_t, sin_t, w, norm_g.reshape(1, WIDTH), dmask, qdec, kdec)


def _hgrn_constants():
    n = HGRN_TILE
    t = np.arange(n)
    tri = (t[:, None] >= t[None, :]).astype(np.float32)
    s = np.arange(HGRN_BLOCK)
    x = s[:, None] ^ s[None, :]
    level = np.where(s[:, None] > s[None, :], np.floor(np.log2(np.maximum(x, 1))), -1)
    return jnp.asarray(tri, BF16), jnp.asarray(level, jnp.int32)


def _hgrn_level_operand(q, k, f, cum2, level):
    n, d = q.shape
    group = 2 << level
    half = group // 2
    sub = V7X_SUBLANES
    if level == 0:
        odd = (lax.broadcasted_iota(jnp.int32, (n, d), 0) & 1) == 1
        return jnp.where(odd, q * f, k)
    if half >= sub:
        shape = (n // group, group, d)
        q3, k3, c3 = q.reshape(shape), k.reshape(shape), cum2.reshape(shape)
        pivot = c3[:, half - 1:half, :]
        lo = k3[:, :half, :] * jnp.exp2(pivot - c3[:, :half, :])
        hi = q3[:, half:, :] * jnp.exp2(c3[:, half:, :] - pivot)
        return jnp.concatenate([lo, hi], axis=1).reshape(n, d)
    shape = (n // sub, sub, d)
    q3, k3, c3 = q.reshape(shape), k.reshape(shape), cum2.reshape(shape)
    row = lax.broadcasted_iota(jnp.int32, shape, 1)
    if group == sub:
        pivot = c3[:, half - 1:half, :]
    else:
        pivot = jnp.where(row < 4, c3[:, 1:2, :], c3[:, 5:6, :])
    second = (row & half) != 0
    return (jnp.where(second, q3, k3) * jnp.exp2(-jnp.abs(c3 - pivot))).reshape(n, d)


def _hgrn_kernel(x_ref, mod_ref, w_ref, lbl_ref, g_ref, tri_ref, lvl_ref, o_ref, state_ref, *, layer):
    @pl.when(pl.program_id(1) == 0)
    def _():
        state_ref[...] = jnp.zeros_like(state_ref)

    logits = lbl_ref[...]
    e = jnp.exp(logits - jnp.max(logits, axis=0, keepdims=True))
    sm = e / jnp.sum(e, axis=0, keepdims=True)
    lb = jnp.zeros((1, WIDTH), F32)
    for j in range(1, layer + 1):
        lb = lb + sm[j:j + 1, :]
    log_lb = jnp.log(lb)
    log_1m_lb = jnp.log(1.0 - lb)

    h = _modulate(x_ref[0], mod_ref, 0).astype(BF16)
    p = _dot(h, w_ref[...])
    gf = p[:, WIDTH:2 * WIDTH]
    log_sig = jnp.minimum(gf, 0.0) - jnp.log(1.0 + jnp.exp(-jnp.abs(gf)))
    b = log_1m_lb + log_sig
    logf = jnp.maximum(log_lb, b) + jnp.log(1.0 + jnp.exp(-jnp.abs(log_lb - b)))
    f_all = jnp.exp(logf)
    tri = tri_ref[...]
    f_hi, f_mid, f_lo = _split3(logf * LOG2_E)
    cum2_all = _dot(tri, f_hi) + _dot(tri, f_mid) + _dot(tri, f_lo)
    lvl = lvl_ref[...]
    blk = HGRN_BLOCK

    for hd in range(N_HEADS):
        lo = hd * HEAD_DIM
        q = _silu(p[:, lo:lo + HEAD_DIM])
        v = p[:, 2 * WIDTH + lo:2 * WIDTH + lo + HEAD_DIM]
        vb = v.astype(BF16)
        gate = _silu(p[:, 3 * WIDTH + lo:3 * WIDTH + lo + HEAD_DIM])
        f = f_all[:, lo:lo + HEAD_DIM]
        k = 1.0 - f
        cum2 = cum2_all[:, lo:lo + HEAD_DIM]
        last2 = cum2[HGRN_TILE - 1:HGRN_TILE, :]

        diag = [None, None]
        for level in range(HGRN_LEVELS - 1):
            m = _hgrn_level_operand(q, k, f, cum2, level).astype(BF16)
            sel = lvl == level
            for bi in range(2):
                mb = m[bi * blk:(bi + 1) * blk]
                pr = _dot_nt(mb, mb)
                diag[bi] = jnp.where(sel, pr, 0.0 if diag[bi] is None else diag[bi])
        m = _hgrn_level_operand(q, k, f, cum2, HGRN_LEVELS - 1).astype(BF16)
        cross = _dot_nt(m[blk:], m[:blk])

        out_lo = _dot(diag[0].astype(BF16), vb[:blk])
        out_hi = _dot(jnp.concatenate([cross, diag[1]], axis=1).astype(BF16), vb)
        out = jnp.concatenate([out_lo, out_hi], axis=0)
        out = out + jnp.sum(q * k, axis=-1, keepdims=True) * v
        state_t = state_ref[hd]
        qe = (q * jnp.exp2(cum2)).astype(BF16)
        out = out + _dot_nt(qe, state_t.astype(BF16))
        ke = (k * jnp.exp2(last2 - cum2)).astype(BF16)
        state_ref[hd] = state_t * jnp.exp2(last2) + _dot_tn(vb, ke)
        ms = jnp.mean(out * out, axis=-1, keepdims=True)
        u = out * lax.rsqrt(ms + LN_EPS) * g_ref[:, lo:lo + HEAD_DIM] * gate
        o_ref[0, :, lo:lo + HEAD_DIM] = u.astype(o_ref.dtype)


def _hgrn_branch(x, mod, w, lb_logits, norm_g, layer):
    bsz, seq, _ = x.shape
    tile = HGRN_TILE
    depth = lb_logits.shape[0]
    tri, lvl = _hgrn_constants()
    row = lambda b, i: (b, i, 0)
    return pl.pallas_call(
        functools.partial(_hgrn_kernel, layer=layer),
        grid=(bsz, seq // tile),
        in_specs=[
            pl.BlockSpec((1, tile, D_MODEL), row),
            pl.BlockSpec((1, N_MOD, D_MODEL), lambda b, i: (b, 0, 0)),
            _const_spec((D_MODEL, 4 * WIDTH)),
            _const_spec((depth, WIDTH)),
            _const_spec((1, WIDTH)),
            _const_spec((tile, tile)),
            _const_spec((HGRN_BLOCK, HGRN_BLOCK)),
        ],
        out_specs=pl.BlockSpec((1, tile, WIDTH), row),
        out_shape=jax.ShapeDtypeStruct((bsz, seq, WIDTH), BF16),
        scratch_shapes=[pltpu.VMEM((N_HEADS, HEAD_DIM, HEAD_DIM), F32)],
        compiler_params=_params(("parallel", "arbitrary")),
        name="hgrn_branch",
    )(x, mod, w, lb_logits.astype(F32), norm_g.reshape(1, WIDTH), tri, lvl)


def _conv_kernel(x_ref, mod_ref, w_ref, cw_ref, cb_ref, lg_ref, lb_ref, o_ref, buf_ref):
    tile = x_ref.shape[1]

    @pl.when(pl.program_id(1) == 0)
    def _():
        buf_ref[0:CONV_HALO, :] = jnp.zeros((CONV_HALO, WIDTH), F32)

    h = _modulate(x_ref[0], mod_ref, 0).astype(BF16)
    p = _dot(h, w_ref[...])
    buf_ref[CONV_HALO:CONV_HALO + tile, :] = p[:, 0:WIDTH] * _sigmoid(p[:, WIDTH:2 * WIDTH])
    first = CONV_HALO - (CONV_KERNEL - 1)
    sub = V7X_SUBLANES
    acc = None
    for s in range(sub):
        offs = [o for o in range(first, first + CONV_KERNEL) if o % sub == s]
        span = tile + (sub if s else 0)
        z = None
        for o in offs:
            term = buf_ref[o - s:o - s + span, :] * cw_ref[o - first:o - first + 1, :]
            z = term if z is None else z + term
        z = z[s:s + tile]
        acc = z + cb_ref[...] if acc is None else acc + z
    tail = buf_ref[tile:tile + CONV_HALO, :]
    buf_ref[0:CONV_HALO, :] = tail
    o_ref[0] = _silu(_layer_norm(acc, lg_ref[...], lb_ref[...])).astype(o_ref.dtype)


def _conv_branch(x, mod, w, conv_w, conv_b, ln_g, ln_b):
    bsz, seq, _ = x.shape
    tile = CONV_TILE
    row = lambda b, i: (b, i, 0)
    return pl.pallas_call(
        _conv_kernel,
        grid=(bsz, seq // tile),
        in_specs=[
            pl.BlockSpec((1, tile, D_MODEL), row),
            pl.BlockSpec((1, N_MOD, D_MODEL), lambda b, i: (b, 0, 0)),
            _const_spec((D_MODEL, 2 * WIDTH)),
            _const_spec((CONV_KERNEL, WIDTH)),
            _const_spec((1, WIDTH)),
            _const_spec((1, WIDTH)),
            _const_spec((1, WIDTH)),
        ],
        out_specs=pl.BlockSpec((1, tile, WIDTH), row),
        out_shape=jax.ShapeDtypeStruct((bsz, seq, WIDTH), BF16),
        scratch_shapes=[pltpu.VMEM((CONV_HALO + tile, WIDTH), F32)],
        compiler_params=_params(("parallel", "arbitrary")),
        name="conv_branch",
    )(x, mod, w, conv_w, conv_b.reshape(1, WIDTH), ln_g.reshape(1, WIDTH), ln_b.reshape(1, WIDTH))


def _merge_kernel(x_ref, mod_ref, ua_ref, ub_ref, uc_ref, wg_ref, wb_ref, wo_ref, lg_ref, lb_ref,
                  o_ref, *, alpha):
    x = x_ref[0]
    h = _modulate(x, mod_ref, 0).astype(BF16)
    y = None
    for i, u_ref in enumerate((ua_ref, ub_ref, uc_ref)):
        gate = _sigmoid(_dot(h, wg_ref[:, i * D_MODEL:(i + 1) * D_MODEL]))
        term = gate * _dot(u_ref[0], wb_ref[i])
        y = term if y is None else y + term
    z = _dot(y.astype(BF16), wo_ref[...])
    res_gate = mod_ref[0, 2:3, :]
    o_ref[0] = _layer_norm(alpha * x + res_gate * z, lg_ref[...], lb_ref[...])


def _merge(x, mod, ua, ub, uc, w_gate, w_branch, w_out, ln_g, ln_b, alpha):
    bsz, seq, _ = x.shape
    tile = MERGE_TILE
    row = lambda b, i: (b, i, 0)
    u_spec = pl.BlockSpec((1, tile, WIDTH), row)
    return pl.pallas_call(
        functools.partial(_merge_kernel, alpha=alpha),
        grid=(bsz, seq // tile),
        in_specs=[
            pl.BlockSpec((1, tile, D_MODEL), row),
            pl.BlockSpec((1, N_MOD, D_MODEL), lambda b, i: (b, 0, 0)),
            u_spec, u_spec, u_spec,
            _const_spec((D_MODEL, N_BRANCH * D_MODEL)),
            _const_spec((N_BRANCH, WIDTH, D_MODEL)),
            _const_spec((D_MODEL, D_MODEL)),
            _const_spec((1, D_MODEL)),
            _const_spec((1, D_MODEL)),
        ],
        out_specs=pl.BlockSpec((1, tile, D_MODEL), row),
        out_shape=jax.ShapeDtypeStruct((bsz, seq, D_MODEL), F32),
        compiler_params=_params(("parallel", "parallel")),
        name="merge_ln",
    )(x, mod, ua, ub, uc, w_gate, w_branch, w_out, ln_g.reshape(1, D_MODEL), ln_b.reshape(1, D_MODEL))


def _ffn_kernel(x_ref, mod_ref, wu_ref, cw_ref, cb_ref, wd_ref, lg_ref, lb_ref,
                o_ref, hext_ref, act_ref, *, alpha):
    tile = x_ref.shape[1]

    @pl.when(pl.program_id(1) == 0)
    def _():
        hext_ref[0:FFN_HALO, :] = jnp.zeros((FFN_HALO, D_MODEL), BF16)

    x = x_ref[0]
    hext_ref[FFN_HALO:FFN_HALO + tile, :] = _modulate(x, mod_ref, 3).astype(BF16)
    hext = hext_ref[...]
    first = FFN_HALO - (FFN_CONV_KERNEL - 1)

    def conv(p, col):
        y = cb_ref[:, col:col + FFN_CHUNK]
        for j in range(FFN_CONV_KERNEL):
            y = y + p[first + j:first + j + tile] * cw_ref[j:j + 1, col:col + FFN_CHUNK]
        return y

    for c in range(D_FF // FFN_CHUNK):
        col_a = c * FFN_CHUNK
        col_v = D_FF + col_a
        pa = _dot(hext, wu_ref[:, col_a:col_a + FFN_CHUNK])
        pv = _dot(hext, wu_ref[:, col_v:col_v + FFN_CHUNK])
        act_ref[:, col_a:col_a + FFN_CHUNK] = (_silu(conv(pa, col_a)) * conv(pv, col_v)).astype(BF16)

    y = _dot(act_ref[...], wd_ref[...])
    hext_ref[0:FFN_HALO, :] = hext_ref[tile:tile + FFN_HALO, :]
    res_gate = mod_ref[0, 5:6, :]
    o_ref[0] = _layer_norm(alpha * x + res_gate * y, lg_ref[...], lb_ref[...])


def _conv_ffn(x, mod, w_up, conv_w, conv_b, w_down, ln_g, ln_b, alpha):
    bsz, seq, _ = x.shape
    tile = FFN_TILE
    row = lambda b, i: (b, i, 0)
    return pl.pallas_call(
        functools.partial(_ffn_kernel, alpha=alpha),
        grid=(bsz, seq // tile),
        in_specs=[
            pl.BlockSpec((1, tile, D_MODEL), row),
            pl.BlockSpec((1, N_MOD, D_MODEL), lambda b, i: (b, 0, 0)),
            _const_spec((D_MODEL, 2 * D_FF)),
            _const_spec((FFN_CONV_KERNEL, 2 * D_FF)),
            _const_spec((1, 2 * D_FF)),
            _const_spec((D_FF, D_MODEL)),
            _const_spec((1, D_MODEL)),
            _const_spec((1, D_MODEL)),
        ],
        out_specs=pl.BlockSpec((1, tile, D_MODEL), row),
        out_shape=jax.ShapeDtypeStruct((bsz, seq, D_MODEL), F32),
        scratch_shapes=[
            pltpu.VMEM((FFN_HALO + tile, D_MODEL), BF16),
            pltpu.VMEM((tile, D_FF), BF16),
        ],
        compiler_params=_params(("parallel", "arbitrary")),
        name="conv_ffn_ln",
    )(x, mod, w_up, conv_w, conv_b.reshape(1, 2 * D_FF), w_down,
      ln_g.reshape(1, D_MODEL), ln_b.reshape(1, D_MODEL))


def kernel(x, c, positions, w_ada, b_ada, w_in, ret_norm_g, hgrn_lb_logits, hgrn_norm_g, conv_w, conv_b,
           conv_ln_g, conv_ln_b, w_branch, w_out, ln1_g, ln1_b, ffn_w_up, ffn_conv_w, ffn_conv_b,
           ffn_w_down, ln2_g, ln2_b):
    depth = w_ada.shape[0]
    bsz = x.shape[0]
    alpha = (2.0 * depth) ** 0.25
    mod_all = _ada_mod(c, w_ada, b_ada)
    cos_t, sin_t = _rope_tables(positions)
    ret_end = 4 * WIDTH
    hgrn_end = ret_end + 4 * WIDTH
    conv_end = hgrn_end + 2 * WIDTH
    for l in range(depth):
        mod = mod_all[l].reshape(bsz, N_MOD, D_MODEL)
        w_l = w_in[l].astype(BF16)
        ua = _retention_branch(x, mod, cos_t, sin_t, w_l[:, :ret_end], ret_norm_g[l])
        ub = _hgrn_branch(x, mod, w_l[:, ret_end:hgrn_end], hgrn_lb_logits, hgrn_norm_g[l], l)
        uc = _conv_branch(x, mod, w_l[:, hgrn_end:conv_end], conv_w[l], conv_b[l], conv_ln_g[l], conv_ln_b[l])
        x = _merge(x, mod, ua, ub, uc, w_l[:, conv_end:], w_branch[l].astype(BF16), w_out[l].astype(BF16),
                   ln1_g[l], ln1_b[l], alpha)
        x = _conv_ffn(x, mod, ffn_w_up[l].astype(BF16), ffn_conv_w[l], ffn_conv_b[l],
                      ffn_w_down[l].astype(BF16), ln2_g[l], ln2_b[l], alpha)
    return x
```

```python
import functools

import numpy as np
import jax
import jax.numpy as jnp
from jax import lax
from jax.experimental import pallas as pl
from jax.experimental.pallas import tpu as pltpu

F32 = jnp.float32
BF16 = jnp.bfloat16

D_MODEL = 1024
N_HEADS = 4
HEAD_DIM = 128
WIDTH = N_HEADS * HEAD_DIM
RET_CHUNK = 128
ROPE_BASE = 10000.0
CONV_KERNEL = 31
N_BRANCH = 3
D_FF = 2816
FFN_CONV_KERNEL = 3
LN_EPS = 1e-5
N_MOD = 6
LOG2_E = 1.4426950408889634

V7X_LANES = 128
V7X_SUBLANES = 8
V7X_BF16_ROWS = 16
V7X_VMEM_LIMIT = 56 * 1024 * 1024

MIX_TILE = 512
HGRN_CHUNK = 256
HGRN_LEVELS = 8
HGRN_BLOCK = HGRN_CHUNK // 2
CONV_HALO = 32
FFN_TILE = 512
FFN_HALO = V7X_BF16_ROWS
FFN_CHUNK = 256
ADA_COLS = 1536


def _sigmoid(x):
    return 0.5 * jnp.tanh(0.5 * x) + 0.5


def _silu(x):
    return x * _sigmoid(x)


def _layer_norm(x, g, b):
    mu = jnp.mean(x, axis=-1, keepdims=True)
    xc = x - mu
    var = jnp.mean(xc * xc, axis=-1, keepdims=True)
    return xc * lax.rsqrt(var + LN_EPS) * g + b


def _dot(a, b):
    return jnp.dot(a, b, preferred_element_type=F32)


def _dot_nt(a, b):
    return lax.dot_general(a, b, (((1,), (1,)), ((), ())), preferred_element_type=F32)


def _dot_tn(a, b):
    return lax.dot_general(a, b, (((0,), (0,)), ((), ())), preferred_element_type=F32)


def _split3(x):
    hi = x.astype(BF16)
    r1 = x - hi.astype(F32)
    mid = r1.astype(BF16)
    lo = (r1 - mid.astype(F32)).astype(BF16)
    return hi, mid, lo


def _modulate(x, mod_ref, row):
    shift = mod_ref[0, row:row + 1, :]
    scale = mod_ref[0, row + 1:row + 2, :]
    return x * (1.0 + scale) + shift


def _params(semantics, flags=None):
    return pltpu.CompilerParams(dimension_semantics=semantics, vmem_limit_bytes=V7X_VMEM_LIMIT,
                                flags=flags)


def _const_spec(shape):
    return pl.BlockSpec(shape, lambda *_: (0,) * len(shape), pipeline_mode=pl.Buffered(1))


def _ada_kernel(c_ref, w_ref, b_ref, o_ref):
    cond = _silu(c_ref[...]).astype(BF16)
    o_ref[0] = _dot(cond, w_ref[0].astype(BF16)) + b_ref[0]


def _ada_mod(c, w_ada, b_ada):
    depth, _, n = w_ada.shape
    bsz = c.shape[0]
    return pl.pallas_call(
        _ada_kernel,
        grid=(depth, n // ADA_COLS),
        in_specs=[
            pl.BlockSpec((bsz, D_MODEL), lambda l, j: (0, 0)),
            pl.BlockSpec((1, D_MODEL, ADA_COLS), lambda l, j: (l, 0, j)),
            pl.BlockSpec((1, 1, ADA_COLS), lambda l, j: (l, 0, j)),
        ],
        out_specs=pl.BlockSpec((1, bsz, ADA_COLS), lambda l, j: (l, 0, j)),
        out_shape=jax.ShapeDtypeStruct((depth, bsz, n), F32),
        compiler_params=_params(("parallel", "parallel")),
        name="ada_mod",
    )(c, w_ada, b_ada.reshape(depth, 1, n))


def _rope_kernel(pos_ref, inv_ref, sign_ref, cos_ref, sin_ref):
    ang = pos_ref[0] * inv_ref[...]
    cos_ref[0] = jnp.cos(ang)
    sin_ref[0] = jnp.sin(ang) * sign_ref[...]


def _rope_tables(positions):
    bsz, seq = positions.shape
    half = HEAD_DIM // 2
    inv = ROPE_BASE ** (-jnp.arange(half, dtype=F32) / half)
    inv = jnp.concatenate([inv, inv]).reshape(1, HEAD_DIM)
    sign = jnp.concatenate([-jnp.ones((half,), F32), jnp.ones((half,), F32)]).reshape(1, HEAD_DIM)
    pos = jnp.broadcast_to(positions.astype(F32)[..., None], (bsz, seq, HEAD_DIM))
    tile = MIX_TILE
    spec = pl.BlockSpec((1, tile, HEAD_DIM), lambda b, i: (b, i, 0))
    return pl.pallas_call(
        _rope_kernel,
        grid=(bsz, seq // tile),
        in_specs=[spec, _const_spec((1, HEAD_DIM)), _const_spec((1, HEAD_DIM))],
        out_specs=[spec, spec],
        out_shape=[jax.ShapeDtypeStruct((bsz, seq, HEAD_DIM), F32)] * 2,
        compiler_params=_params(("parallel", "parallel")),
        name="rope_tables",
    )(pos, inv, sign)


def _retention_constants():
    c = RET_CHUNK
    log_gamma = np.log1p(-np.exp2(-5.0 - np.arange(N_HEADS, dtype=np.float64)))
    idx = np.arange(c, dtype=np.float64)
    rel = idx[:, None] - idx[None, :]
    mask = np.where(rel >= 0, np.exp(log_gamma[:, None, None] * np.maximum(rel, 0.0)), 0.0)
    qdec = np.exp(log_gamma[:, None] * (idx + 1.0))
    kdec = np.exp(log_gamma[:, None] * (c - 1.0 - idx))
    ones = np.ones((1, 1, HEAD_DIM))
    cdec = tuple(float(v) for v in np.exp(log_gamma * c))
    return (jnp.asarray(mask, F32), jnp.asarray(qdec[:, :, None] * ones, F32),
            jnp.asarray(kdec[:, :, None] * ones, F32), cdec)


def _retention_rope(hd, pq, pk, cos_ref, sin_ref):
    lo = hd * HEAD_DIM
    cosf = cos_ref[0]
    sinf = sin_ref[0]
    q = pq[:, lo:lo + HEAD_DIM]
    k = pk[:, lo:lo + HEAD_DIM]
    q = q * cosf + pltpu.roll(q, HEAD_DIM // 2, 1) * sinf
    k = (k * cosf + pltpu.roll(k, HEAD_DIM // 2, 1) * sinf) * (HEAD_DIM ** -0.5)
    return q, k


def _retention_chunk(hd, c, q, k, pv, pg, state, g_ref, dm_ref, qd_ref, kd_ref, chunk_decay):
    lo = hd * HEAD_DIM
    r0 = c * RET_CHUNK
    qc = q[r0:r0 + RET_CHUNK].astype(BF16)
    kf = k[r0:r0 + RET_CHUNK]
    kc = kf.astype(BF16)
    vc = pv[r0:r0 + RET_CHUNK, lo:lo + HEAD_DIM].astype(BF16)
    scores = _dot_nt(qc, kc) * dm_ref[hd]
    out = _dot(scores.astype(BF16), vc) + _dot(qc, state.astype(BF16)) * qd_ref[hd]
    new_state = chunk_decay[hd] * state + _dot_tn((kf * kd_ref[hd]).astype(BF16), vc)
    mu = jnp.mean(out, axis=-1, keepdims=True)
    oc = out - mu
    var = jnp.mean(oc * oc, axis=-1, keepdims=True)
    gate = _silu(pg[r0:r0 + RET_CHUNK, lo:lo + HEAD_DIM])
    u = oc * lax.rsqrt(var + LN_EPS) * g_ref[:, lo:lo + HEAD_DIM] * gate
    return u.astype(BF16), new_state


def _hgrn_constants():
    n = HGRN_CHUNK
    t = np.arange(n)
    tri = (t[:, None] >= t[None, :]).astype(np.float32)
    s = np.arange(HGRN_BLOCK)
    x = s[:, None] ^ s[None, :]
    level = np.where(s[:, None] > s[None, :], np.floor(np.log2(np.maximum(x, 1))), -1)
    return jnp.asarray(tri, BF16), jnp.asarray(level, jnp.int32)


def _hgrn_level_operand(q, k, f, cum2, level):
    n, d = q.shape
    group = 2 << level
    half = group // 2
    sub = V7X_SUBLANES
    if level == 0:
        odd = (lax.broadcasted_iota(jnp.int32, (n, d), 0) & 1) == 1
        return jnp.where(odd, q * f, k)
    if half >= sub:
        shape = (n // group, group, d)
        q3, k3, c3 = q.reshape(shape), k.reshape(shape), cum2.reshape(shape)
        pivot = c3[:, half - 1:half, :]
        lo = k3[:, :half, :] * jnp.exp2(pivot - c3[:, :half, :])
        hi = q3[:, half:, :] * jnp.exp2(c3[:, half:, :] - pivot)
        return jnp.concatenate([lo, hi], axis=1).reshape(n, d)
    shape = (n // sub, sub, d)
    q3, k3, c3 = q.reshape(shape), k.reshape(shape), cum2.reshape(shape)
    row = lax.broadcasted_iota(jnp.int32, shape, 1)
    if group == sub:
        pivot = c3[:, half - 1:half, :]
    else:
        pivot = jnp.where(row < 4, c3[:, 1:2, :], c3[:, 5:6, :])
    second = (row & half) != 0
    return (jnp.where(second, q3, k3) * jnp.exp2(-jnp.abs(c3 - pivot))).reshape(n, d)


def _hgrn_decay(gf, lbl_ref, tri_ref, layer):
    logits = lbl_ref[...]
    e = jnp.exp(logits - jnp.max(logits, axis=0, keepdims=True))
    sm = e / jnp.sum(e, axis=0, keepdims=True)
    lb = jnp.zeros((1, WIDTH), F32)
    for j in range(1, layer + 1):
        lb = lb + sm[j:j + 1, :]
    log_lb = jnp.log(lb)
    log_1m_lb = jnp.log(1.0 - lb)
    log_sig = jnp.minimum(gf, 0.0) - jnp.log(1.0 + jnp.exp(-jnp.abs(gf)))
    b = log_1m_lb + log_sig
    logf = jnp.maximum(log_lb, b) + jnp.log(1.0 + jnp.exp(-jnp.abs(log_lb - b)))
    tri = tri_ref[...]
    f_hi, f_mid, f_lo = _split3(logf * LOG2_E)
    cum2 = _dot(tri, f_hi) + _dot(tri, f_mid) + _dot(tri, f_lo)
    return jnp.exp(logf), cum2


def _hgrn_head(hd, r0, pq, pi, pg, f_all, cum2_all, state_t, g_ref, lvl_ref):
    lo = hd * HEAD_DIM
    blk = HGRN_BLOCK
    lvl = lvl_ref[...]
    q = _silu(pq[r0:r0 + HGRN_CHUNK, lo:lo + HEAD_DIM])
    v = pi[r0:r0 + HGRN_CHUNK, lo:lo + HEAD_DIM]
    vb = v.astype(BF16)
    gate = _silu(pg[r0:r0 + HGRN_CHUNK, lo:lo + HEAD_DIM])
    f = f_all[:, lo:lo + HEAD_DIM]
    k = 1.0 - f
    cum2 = cum2_all[:, lo:lo + HEAD_DIM]
    last2 = cum2[HGRN_CHUNK - 1:HGRN_CHUNK, :]

    diag = [None, None]
    for level in range(HGRN_LEVELS - 1):
        m = _hgrn_level_operand(q, k, f, cum2, level).astype(BF16)
        sel = lvl == level
        for bi in range(2):
            mb = m[bi * blk:(bi + 1) * blk]
            pr = _dot_nt(mb, mb)
            diag[bi] = jnp.where(sel, pr, 0.0 if diag[bi] is None else diag[bi])
    m = _hgrn_level_operand(q, k, f, cum2, HGRN_LEVELS - 1).astype(BF16)
    cross = _dot_nt(m[blk:], m[:blk])

    out_lo = _dot(diag[0].astype(BF16), vb[:blk])
    out_hi = _dot(jnp.concatenate([cross, diag[1]], axis=1).astype(BF16), vb)
    out = jnp.concatenate([out_lo, out_hi], axis=0)
    out = out + jnp.sum(q * k, axis=-1, keepdims=True) * v
    qe = (q * jnp.exp2(cum2)).astype(BF16)
    out = out + _dot_nt(qe, state_t.astype(BF16))
    ke = (k * jnp.exp2(last2 - cum2)).astype(BF16)
    new_state = state_t * jnp.exp2(last2) + _dot_tn(vb, ke)
    ms = jnp.mean(out * out, axis=-1, keepdims=True)
    u = out * lax.rsqrt(ms + LN_EPS) * g_ref[:, lo:lo + HEAD_DIM] * gate
    return u.astype(BF16), new_state


def _causal_conv31(ext, tile, cw_ref, cb_ref):
    first = CONV_HALO - (CONV_KERNEL - 1)
    sub = V7X_SUBLANES
    acc = None
    for s in range(sub):
        offs = [o for o in range(first, first + CONV_KERNEL) if o % sub == s]
        span = tile + (sub if s else 0)
        z = None
        for o in offs:
            term = ext[o - s:o - s + span] * cw_ref[o - first:o - first + 1, :]
            z = term if z is None else z + term
        z = z[s:s + tile]
        acc = z + cb_ref[...] if acc is None else acc + z
    return acc


def _mixer_kernel(x_ref, mod_ref, cos_ref, sin_ref,
                  w_ret_ref, ret_g_ref, dm_ref, qd_ref, kd_ref,
                  w_hgrn_ref, lbl_ref, hgrn_g_ref, tri_ref, lvl_ref,
                  w_conv_ref, cw_ref, cb_ref, clg_ref, clb_ref,
                  wg_ref, wb_ref, wo_ref, lg_ref, lb_ref,
                  o_ref,
                  ret_state_ref, hgrn_state_ref, conv_hist_ref,
                  *, layer, alpha, chunk_decay):
    @pl.when(pl.program_id(1) == 0)
    def _():
        ret_state_ref[...] = jnp.zeros_like(ret_state_ref)
        hgrn_state_ref[...] = jnp.zeros_like(hgrn_state_ref)
        conv_hist_ref[...] = jnp.zeros_like(conv_hist_ref)

    tile = x_ref.shape[1]
    x = x_ref[0]
    h = _modulate(x, mod_ref, 0).astype(BF16)
    seg = lambda w_ref, j: _dot(h, w_ref[:, j * WIDTH:(j + 1) * WIDTH])

    glu = seg(w_conv_ref, 0) * _sigmoid(seg(w_conv_ref, 1))
    ext = jnp.concatenate([conv_hist_ref[...], glu], axis=0)
    conv_hist_ref[...] = glu[tile - CONV_HALO:]
    conv = _causal_conv31(ext, tile, cw_ref, cb_ref)
    u_c = _silu(_layer_norm(conv, clg_ref[...], clb_ref[...])).astype(BF16)

    rq, rk, rv, rg = (seg(w_ret_ref, j) for j in range(4))
    rope = [_retention_rope(hd, rq, rk, cos_ref, sin_ref) for hd in range(N_HEADS)]
    ret_state = [ret_state_ref[hd] for hd in range(N_HEADS)]
    ret_rows = []
    for c in range(tile // RET_CHUNK):
        heads = []
        for hd in range(N_HEADS):
            u, ret_state[hd] = _retention_chunk(hd, c, rope[hd][0], rope[hd][1], rv, rg, ret_state[hd],
                                                ret_g_ref, dm_ref, qd_ref, kd_ref, chunk_decay)
            heads.append(u)
        ret_rows.append(jnp.concatenate(heads, axis=1))
    for hd in range(N_HEADS):
        ret_state_ref[hd] = ret_state[hd]
    u_a = jnp.concatenate(ret_rows, axis=0)

    gq, gf, gi, gg = (seg(w_hgrn_ref, j) for j in range(4))
    hgrn_state = [hgrn_state_ref[hd] for hd in range(N_HEADS)]
    hgrn_rows = []
    for c in range(tile // HGRN_CHUNK):
        r0 = c * HGRN_CHUNK
        f_all, cum2_all = _hgrn_decay(gf[r0:r0 + HGRN_CHUNK], lbl_ref, tri_ref, layer)
        heads = []
        for hd in range(N_HEADS):
            u, hgrn_state[hd] = _hgrn_head(hd, r0, gq, gi, gg, f_all, cum2_all, hgrn_state[hd],
                                           hgrn_g_ref, lvl_ref)
            heads.append(u)
        hgrn_rows.append(jnp.concatenate(heads, axis=1))
    for hd in range(N_HEADS):
        hgrn_state_ref[hd] = hgrn_state[hd]
    u_b = jnp.concatenate(hgrn_rows, axis=0)

    y = None
    for i, u in enumerate((u_a, u_b, u_c)):
        gate = _sigmoid(_dot(h, wg_ref[:, i * D_MODEL:(i + 1) * D_MODEL]))
        term = gate * _dot(u, wb_ref[i])
        y = term if y is None else y + term
    z = _dot(y.astype(BF16), wo_ref[...])
    o_ref[0] = _layer_norm(alpha * x + mod_ref[0, 2:3, :] * z, lg_ref[...], lb_ref[...])


def _token_mixer(x, mod, cos_t, sin_t, w_ret, ret_norm_g, w_hgrn, lb_logits, hgrn_norm_g,
                 w_conv, conv_w, conv_b, conv_ln_g, conv_ln_b, w_gate, w_branch, w_out,
                 ln_g, ln_b, layer, alpha):
    bsz, seq, _ = x.shape
    tile = MIX_TILE
    assert tile == 2 * HGRN_CHUNK and seq % tile == 0
    depth = lb_logits.shape[0]
    dmask, qdec, kdec, cdec = _retention_constants()
    tri, lvl = _hgrn_constants()
    row = lambda b, i: (b, i, 0)
    vec = lambda a, n: a.reshape(1, n)
    return pl.pallas_call(
        functools.partial(_mixer_kernel, layer=layer, alpha=alpha, chunk_decay=cdec),
        grid=(bsz, seq // tile),
        in_specs=[
            pl.BlockSpec((1, tile, D_MODEL), row),
            pl.BlockSpec((1, N_MOD, D_MODEL), lambda b, i: (b, 0, 0)),
            pl.BlockSpec((1, tile, HEAD_DIM), row),
            pl.BlockSpec((1, tile, HEAD_DIM), row),
            _const_spec((D_MODEL, 4 * WIDTH)),
            _const_spec((1, WIDTH)),
            _const_spec((N_HEADS, RET_CHUNK, RET_CHUNK)),
            _const_spec((N_HEADS, RET_CHUNK, HEAD_DIM)),
            _const_spec((N_HEADS, RET_CHUNK, HEAD_DIM)),
            _const_spec((D_MODEL, 4 * WIDTH)),
            _const_spec((depth, WIDTH)),
            _const_spec((1, WIDTH)),
            _const_spec((HGRN_CHUNK, HGRN_CHUNK)),
            _const_spec((HGRN_BLOCK, HGRN_BLOCK)),
            _const_spec((D_MODEL, 2 * WIDTH)),
            _const_spec((CONV_KERNEL, WIDTH)),
            _const_spec((1, WIDTH)),
            _const_spec((1, WIDTH)),
            _const_spec((1, WIDTH)),
            _const_spec((D_MODEL, N_BRANCH * D_MODEL)),
            _const_spec((N_BRANCH, WIDTH, D_MODEL)),
            _const_spec((D_MODEL, D_MODEL)),
            _const_spec((1, D_MODEL)),
            _const_spec((1, D_MODEL)),
        ],
        out_specs=pl.BlockSpec((1, tile, D_MODEL), row),
        out_shape=jax.ShapeDtypeStruct((bsz, seq, D_MODEL), F32),
        scratch_shapes=[
            pltpu.VMEM((N_HEADS, HEAD_DIM, HEAD_DIM), F32),
            pltpu.VMEM((N_HEADS, HEAD_DIM, HEAD_DIM), F32),
            pltpu.VMEM((CONV_HALO, WIDTH), F32),
        ],
        compiler_params=_params(("parallel", "arbitrary")),
        name="token_mixer",
    )(x, mod, cos_t, sin_t,
      w_ret, vec(ret_norm_g, WIDTH), dmask, qdec, kdec,
      w_hgrn, lb_logits.astype(F32), vec(hgrn_norm_g, WIDTH), tri, lvl,
      w_conv, conv_w, vec(conv_b, WIDTH), vec(conv_ln_g, WIDTH), vec(conv_ln_b, WIDTH),
      w_gate, w_branch, w_out, vec(ln_g, D_MODEL), vec(ln_b, D_MODEL))


def _ffn_kernel(x_ref, mod_ref, wu_ref, cw_ref, cb_ref, wd_ref, lg_ref, lb_ref,
                o_ref, hext_ref, act_ref, *, alpha):
    tile = x_ref.shape[1]

    @pl.when(pl.program_id(1) == 0)
    def _():
        hext_ref[0:FFN_HALO, :] = jnp.zeros((FFN_HALO, D_MODEL), BF16)

    x = x_ref[0]
    hext_ref[FFN_HALO:FFN_HALO + tile, :] = _modulate(x, mod_ref, 3).astype(BF16)
    hext = hext_ref[...]
    first = FFN_HALO - (FFN_CONV_KERNEL - 1)

    def conv(p, col):
        y = cb_ref[:, col:col + FFN_CHUNK]
        for j in range(FFN_CONV_KERNEL):
            y = y + p[first + j:first + j + tile] * cw_ref[j:j + 1, col:col + FFN_CHUNK]
        return y

    for c in range(D_FF // FFN_CHUNK):
        col_a = c * FFN_CHUNK
        col_v = D_FF + col_a
        pa = _dot(hext, wu_ref[:, col_a:col_a + FFN_CHUNK])
        pv = _dot(hext, wu_ref[:, col_v:col_v + FFN_CHUNK])
        act_ref[:, col_a:col_a + FFN_CHUNK] = (_silu(conv(pa, col_a)) * conv(pv, col_v)).astype(BF16)

    hext_ref[0:FFN_HALO, :] = hext_ref[tile:tile + FFN_HALO, :]
    res_gate = mod_ref[0, 5:6, :]
    for r in range(2):
        rows = slice(r * tile // 2, (r + 1) * tile // 2)
        y = _dot(act_ref[rows, :], wd_ref[...])
        o_ref[0, rows, :] = _layer_norm(alpha * x[rows] + res_gate * y, lg_ref[...], lb_ref[...])


def _conv_ffn(x, mod, w_up, conv_w, conv_b, w_down, ln_g, ln_b, alpha):
    bsz, seq, _ = x.shape
    tile = FFN_TILE
    row = lambda b, i: (b, i, 0)
    return pl.pallas_call(
        functools.partial(_ffn_kernel, alpha=alpha),
        grid=(bsz, seq // tile),
        in_specs=[
            pl.BlockSpec((1, tile, D_MODEL), row),
            pl.BlockSpec((1, N_MOD, D_MODEL), lambda b, i: (b, 0, 0)),
            _const_spec((D_MODEL, 2 * D_FF)),
            _const_spec((FFN_CONV_KERNEL, 2 * D_FF)),
            _const_spec((1, 2 * D_FF)),
            _const_spec((D_FF, D_MODEL)),
            _const_spec((1, D_MODEL)),
            _const_spec((1, D_MODEL)),
        ],
        out_specs=pl.BlockSpec((1, tile, D_MODEL), row),
        out_shape=jax.ShapeDtypeStruct((bsz, seq, D_MODEL), F32),
        scratch_shapes=[
            pltpu.VMEM((FFN_HALO + tile, D_MODEL), BF16),
            pltpu.VMEM((tile, D_FF), BF16),
        ],
        compiler_params=_params(("parallel", "arbitrary")),
        name="conv_ffn_ln",
    )(x, mod, w_up, conv_w, conv_b.reshape(1, 2 * D_FF), w_down,
      ln_g.reshape(1, D_MODEL), ln_b.reshape(1, D_MODEL))


def kernel(x, c, positions, w_ada, b_ada, w_in, ret_norm_g, hgrn_lb_logits, hgrn_norm_g, conv_w, conv_b,
           conv_ln_g, conv_ln_b, w_branch, w_out, ln1_g, ln1_b, ffn_w_up, ffn_conv_w, ffn_conv_b,
           ffn_w_down, ln2_g, ln2_b):
    depth = w_ada.shape[0]
    bsz = x.shape[0]
    alpha = (2.0 * depth) ** 0.25
    mod_all = _ada_mod(c, w_ada, b_ada)
    cos_t, sin_t = _rope_tables(positions)
    ret_end = 4 * WIDTH
    hgrn_end = ret_end + 4 * WIDTH
    conv_end = hgrn_end + 2 * WIDTH
    for l in range(depth):
        mod = mod_all[l].reshape(bsz, N_MOD, D_MODEL)
        w_l = w_in[l].astype(BF16)
        x = _token_mixer(x, mod, cos_t, sin_t,
                         w_l[:, :ret_end], ret_norm_g[l],
                         w_l[:, ret_end:hgrn_end], hgrn_lb_logits, hgrn_norm_g[l],
                         w_l[:, hgrn_end:conv_end], conv_w[l], conv_b[l], conv_ln_g[l], conv_ln_b[l],
                         w_l[:, conv_end:], w_branch[l].astype(BF16), w_out[l].astype(BF16),
                         ln1_g[l], ln1_b[l], l, alpha)
        x = _conv_ffn(x, mod, ffn_w_up[l].astype(BF16), ffn_conv_w[l], ffn_conv_b[l],
                      ffn_w_down[l].astype(BF16), ln2_g[l], ln2_b[l], alpha)
    return x
```

```python
import functools

import numpy as np
import jax
import jax.numpy as jnp
from jax import lax
from jax.experimental import pallas as pl
from jax.experimental.pallas import tpu as pltpu

F32 = jnp.float32
BF16 = jnp.bfloat16

D_MODEL = 1024
N_HEADS = 4
HEAD_DIM = 128
WIDTH = N_HEADS * HEAD_DIM
RET_CHUNK = 256
ROPE_BASE = 10000.0
CONV_KERNEL = 31
N_BRANCH = 3
D_FF = 2816
FFN_CONV_KERNEL = 3
LN_EPS = 1e-5
N_MOD = 6
LOG2_E = 1.4426950408889634

V7X_LANES = 128
V7X_SUBLANES = 8
V7X_BF16_ROWS = 16
V7X_VMEM_LIMIT = 56 * 1024 * 1024

MIX_TILE = 512
HGRN_CHUNK = 256
HGRN_LEVELS = 8
HGRN_BLOCK = HGRN_CHUNK // 2
CONV_HALO = 32
FFN_TILE = 512
FFN_HALO = V7X_BF16_ROWS
FFN_CHUNK = 256
ADA_COLS = 1536


def _sigmoid(x):
    return 0.5 * jnp.tanh(0.5 * x) + 0.5


def _silu(x):
    return x * _sigmoid(x)


def _layer_norm(x, g, b):
    mu = jnp.mean(x, axis=-1, keepdims=True)
    xc = x - mu
    var = jnp.mean(xc * xc, axis=-1, keepdims=True)
    return xc * lax.rsqrt(var + LN_EPS) * g + b


def _dot(a, b):
    return jnp.dot(a, b, preferred_element_type=F32)


def _dot_nt(a, b):
    return lax.dot_general(a, b, (((1,), (1,)), ((), ())), preferred_element_type=F32)


def _dot_tn(a, b):
    return lax.dot_general(a, b, (((0,), (0,)), ((), ())), preferred_element_type=F32)


def _split3(x):
    hi = x.astype(BF16)
    r1 = x - hi.astype(F32)
    mid = r1.astype(BF16)
    lo = (r1 - mid.astype(F32)).astype(BF16)
    return hi, mid, lo


def _modulate(x, mod_ref, row):
    shift = mod_ref[0, row:row + 1, :]
    scale = mod_ref[0, row + 1:row + 2, :]
    return x * (1.0 + scale) + shift


def _params(semantics, flags=None):
    return pltpu.CompilerParams(dimension_semantics=semantics, vmem_limit_bytes=V7X_VMEM_LIMIT,
                                flags=flags)


def _const_spec(shape, index=None):
    index = (0,) * len(shape) if index is None else index
    return pl.BlockSpec(shape, lambda *_: index, pipeline_mode=pl.Buffered(1))


def _ada_kernel(c_ref, w_ref, b_ref, o_ref):
    cond = _silu(c_ref[...]).astype(BF16)
    o_ref[0] = _dot(cond, w_ref[0].astype(BF16)) + b_ref[0]


def _ada_mod(c, w_ada, b_ada):
    depth, _, n = w_ada.shape
    bsz = c.shape[0]
    return pl.pallas_call(
        _ada_kernel,
        grid=(depth, n // ADA_COLS),
        in_specs=[
            pl.BlockSpec((bsz, D_MODEL), lambda l, j: (0, 0)),
            pl.BlockSpec((1, D_MODEL, ADA_COLS), lambda l, j: (l, 0, j)),
            pl.BlockSpec((1, 1, ADA_COLS), lambda l, j: (l, 0, j)),
        ],
        out_specs=pl.BlockSpec((1, bsz, ADA_COLS), lambda l, j: (l, 0, j)),
        out_shape=jax.ShapeDtypeStruct((depth, bsz, n), F32),
        compiler_params=_params(("parallel", "parallel")),
        name="ada_mod",
    )(c, w_ada, b_ada.reshape(depth, 1, n))


def _rope_kernel(pos_ref, inv_ref, sign_ref, cos_ref, sin_ref):
    ang = pos_ref[0] * inv_ref[...]
    cos_ref[0] = jnp.cos(ang)
    sin_ref[0] = jnp.sin(ang) * sign_ref[...]


def _rope_tables(positions):
    bsz, seq = positions.shape
    half = HEAD_DIM // 2
    inv = ROPE_BASE ** (-jnp.arange(half, dtype=F32) / half)
    inv = jnp.concatenate([inv, inv]).reshape(1, HEAD_DIM)
    sign = jnp.concatenate([-jnp.ones((half,), F32), jnp.ones((half,), F32)]).reshape(1, HEAD_DIM)
    pos = jnp.broadcast_to(positions.astype(F32)[..., None], (bsz, seq, HEAD_DIM))
    tile = MIX_TILE
    spec = pl.BlockSpec((1, tile, HEAD_DIM), lambda b, i: (b, i, 0))
    return pl.pallas_call(
        _rope_kernel,
        grid=(bsz, seq // tile),
        in_specs=[spec, _const_spec((1, HEAD_DIM)), _const_spec((1, HEAD_DIM))],
        out_specs=[spec, spec],
        out_shape=[jax.ShapeDtypeStruct((bsz, seq, HEAD_DIM), F32)] * 2,
        compiler_params=_params(("parallel", "parallel")),
        name="rope_tables",
    )(pos, inv, sign)


def _retention_constants():
    c = RET_CHUNK
    log_gamma = np.log1p(-np.exp2(-5.0 - np.arange(N_HEADS, dtype=np.float64)))
    idx = np.arange(c, dtype=np.float64)
    rel = idx[:, None] - idx[None, :]
    mask = np.where(rel >= 0, np.exp(log_gamma[:, None, None] * np.maximum(rel, 0.0)), 0.0)
    qdec = np.exp(log_gamma[:, None] * (idx + 1.0))
    kdec = np.exp(log_gamma[:, None] * (c - 1.0 - idx))
    ones = np.ones((1, 1, HEAD_DIM))
    cdec = tuple(float(v) for v in np.exp(log_gamma * c))
    return (jnp.asarray(mask, F32), jnp.asarray(qdec[:, :, None] * ones, F32),
            jnp.asarray(kdec[:, :, None] * ones, F32), cdec)


def _retention_rope(hd, pq, pk, cos_ref, sin_ref):
    lo = hd * HEAD_DIM
    cosf = cos_ref[0]
    sinf = sin_ref[0]
    q = pq[:, lo:lo + HEAD_DIM]
    k = pk[:, lo:lo + HEAD_DIM]
    q = q * cosf + pltpu.roll(q, HEAD_DIM // 2, 1) * sinf
    k = (k * cosf + pltpu.roll(k, HEAD_DIM // 2, 1) * sinf) * (HEAD_DIM ** -0.5)
    return q, k


def _retention_chunk(hd, c, q, k, pv, pg, state, g_ref, dm_ref, qd_ref, kd_ref, chunk_decay):
    lo = hd * HEAD_DIM
    r0 = c * RET_CHUNK
    qc = q[r0:r0 + RET_CHUNK].astype(BF16)
    kf = k[r0:r0 + RET_CHUNK]
    kc = kf.astype(BF16)
    vc = pv[r0:r0 + RET_CHUNK, lo:lo + HEAD_DIM].astype(BF16)
    scores = _dot_nt(qc, kc) * dm_ref[hd]
    out = _dot(scores.astype(BF16), vc) + _dot(qc, state.astype(BF16)) * qd_ref[hd]
    new_state = chunk_decay[hd] * state + _dot_tn((kf * kd_ref[hd]).astype(BF16), vc)
    mu = jnp.mean(out, axis=-1, keepdims=True)
    oc = out - mu
    var = jnp.mean(oc * oc, axis=-1, keepdims=True)
    gate = _silu(pg[r0:r0 + RET_CHUNK, lo:lo + HEAD_DIM])
    u = oc * lax.rsqrt(var + LN_EPS) * g_ref[:, lo:lo + HEAD_DIM] * gate
    return u.astype(BF16), new_state


def _hgrn_constants():
    n = HGRN_CHUNK
    t = np.arange(n)
    tri = (t[:, None] >= t[None, :]).astype(np.float32)
    s = np.arange(HGRN_BLOCK)
    x = s[:, None] ^ s[None, :]
    level = np.where(s[:, None] > s[None, :], np.floor(np.log2(np.maximum(x, 1))), -1)
    return jnp.asarray(tri, BF16), jnp.asarray(level, jnp.int32)


def _hgrn_level_operand(q, k, f, cum2, level):
    n, d = q.shape
    group = 2 << level
    half = group // 2
    sub = V7X_SUBLANES
    if level == 0:
        odd = (lax.broadcasted_iota(jnp.int32, (n, d), 0) & 1) == 1
        return jnp.where(odd, q * f, k)
    if half >= sub:
        shape = (n // group, group, d)
        q3, k3, c3 = q.reshape(shape), k.reshape(shape), cum2.reshape(shape)
        pivot = c3[:, half - 1:half, :]
        lo = k3[:, :half, :] * jnp.exp2(pivot - c3[:, :half, :])
        hi = q3[:, half:, :] * jnp.exp2(c3[:, half:, :] - pivot)
        return jnp.concatenate([lo, hi], axis=1).reshape(n, d)
    shape = (n // sub, sub, d)
    q3, k3, c3 = q.reshape(shape), k.reshape(shape), cum2.reshape(shape)
    row = lax.broadcasted_iota(jnp.int32, shape, 1)
    if group == sub:
        pivot = c3[:, half - 1:half, :]
    else:
        pivot = jnp.where(row < 4, c3[:, 1:2, :], c3[:, 5:6, :])
    second = (row & half) != 0
    return (jnp.where(second, q3, k3) * jnp.exp2(-jnp.abs(c3 - pivot))).reshape(n, d)


def _hgrn_decay(gf, lbl_ref, tri_ref, layer):
    logits = lbl_ref[...]
    e = jnp.exp(logits - jnp.max(logits, axis=0, keepdims=True))
    sm = e / jnp.sum(e, axis=0, keepdims=True)
    lb = jnp.zeros((1, WIDTH), F32)
    for j in range(1, layer + 1):
        lb = lb + sm[j:j + 1, :]
    log_lb = jnp.log(lb)
    log_1m_lb = jnp.log(1.0 - lb)
    log_sig = jnp.minimum(gf, 0.0) - jnp.log(1.0 + jnp.exp(-jnp.abs(gf)))
    b = log_1m_lb + log_sig
    logf = jnp.maximum(log_lb, b) + jnp.log(1.0 + jnp.exp(-jnp.abs(log_lb - b)))
    tri = tri_ref[...]
    f_hi, f_mid, f_lo = _split3(logf * LOG2_E)
    cum2 = _dot(tri, f_hi) + _dot(tri, f_mid) + _dot(tri, f_lo)
    return jnp.exp(logf), cum2


def _hgrn_head(hd, r0, pq, pi, pg, f_all, cum2_all, state_t, g_ref, lvl_ref):
    lo = hd * HEAD_DIM
    blk = HGRN_BLOCK
    lvl = lvl_ref[...]
    q = _silu(pq[r0:r0 + HGRN_CHUNK, lo:lo + HEAD_DIM])
    v = pi[r0:r0 + HGRN_CHUNK, lo:lo + HEAD_DIM]
    vb = v.astype(BF16)
    gate = _silu(pg[r0:r0 + HGRN_CHUNK, lo:lo + HEAD_DIM])
    f = f_all[:, lo:lo + HEAD_DIM]
    k = 1.0 - f
    cum2 = cum2_all[:, lo:lo + HEAD_DIM]
    last2 = cum2[HGRN_CHUNK - 1:HGRN_CHUNK, :]

    diag = [None, None]
    for level in range(HGRN_LEVELS - 1):
        m = _hgrn_level_operand(q, k, f, cum2, level).astype(BF16)
        sel = lvl == level
        for bi in range(2):
            mb = m[bi * blk:(bi + 1) * blk]
            pr = _dot_nt(mb, mb)
            diag[bi] = jnp.where(sel, pr, 0.0 if diag[bi] is None else diag[bi])
    m = _hgrn_level_operand(q, k, f, cum2, HGRN_LEVELS - 1).astype(BF16)
    cross = _dot_nt(m[blk:], m[:blk])

    out_lo = _dot(diag[0].astype(BF16), vb[:blk])
    out_hi = _dot(jnp.concatenate([cross, diag[1]], axis=1).astype(BF16), vb)
    out = jnp.concatenate([out_lo, out_hi], axis=0)
    out = out + jnp.sum(q * k, axis=-1, keepdims=True) * v
    qe = (q * jnp.exp2(cum2)).astype(BF16)
    out = out + _dot_nt(qe, state_t.astype(BF16))
    ke = (k * jnp.exp2(last2 - cum2)).astype(BF16)
    new_state = state_t * jnp.exp2(last2) + _dot_tn(vb, ke)
    ms = jnp.mean(out * out, axis=-1, keepdims=True)
    u = out * lax.rsqrt(ms + LN_EPS) * g_ref[:, lo:lo + HEAD_DIM] * gate
    return u.astype(BF16), new_state


def _causal_conv31(ext, tile, cw_ref, cb_ref):
    first = CONV_HALO - (CONV_KERNEL - 1)
    sub = V7X_SUBLANES
    acc = None
    for s in range(sub):
        offs = [o for o in range(first, first + CONV_KERNEL) if o % sub == s]
        span = tile + (sub if s else 0)
        z = None
        for o in offs:
            term = ext[o - s:o - s + span] * cw_ref[o - first:o - first + 1, :]
            z = term if z is None else z + term
        z = z[s:s + tile]
        acc = z + cb_ref[...] if acc is None else acc + z
    return acc


def _mixer_kernel(x_ref, mod_ref, cos_ref, sin_ref,
                  w_ret_ref, ret_g_ref, dm_ref, qd_ref, kd_ref,
                  w_hgrn_ref, lbl_ref, hgrn_g_ref, tri_ref, lvl_ref,
                  w_conv_ref, cw_ref, cb_ref, clg_ref, clb_ref,
                  wg0_ref, wg1_ref, wg2_ref, wb_ref, wo_ref, lg_ref, lb_ref,
                  o_ref,
                  ret_state_ref, hgrn_state_ref, conv_hist_ref,
                  *, layer, alpha, chunk_decay):
    @pl.when(pl.program_id(1) == 0)
    def _():
        ret_state_ref[...] = jnp.zeros_like(ret_state_ref)
        hgrn_state_ref[...] = jnp.zeros_like(hgrn_state_ref)
        conv_hist_ref[...] = jnp.zeros_like(conv_hist_ref)

    tile = x_ref.shape[1]
    x = x_ref[0]
    h = _modulate(x, mod_ref, 0).astype(BF16)
    seg = lambda w_ref, j: _dot(h, w_ref[:, j * WIDTH:(j + 1) * WIDTH])

    glu = seg(w_conv_ref, 0) * _sigmoid(seg(w_conv_ref, 1))
    ext = jnp.concatenate([conv_hist_ref[...], glu], axis=0)
    conv_hist_ref[...] = glu[tile - CONV_HALO:]
    conv = _causal_conv31(ext, tile, cw_ref, cb_ref)
    u_c = _silu(_layer_norm(conv, clg_ref[...], clb_ref[...])).astype(BF16)

    rq, rk, rv, rg = (seg(w_ret_ref, j) for j in range(4))
    rope = [_retention_rope(hd, rq, rk, cos_ref, sin_ref) for hd in range(N_HEADS)]
    ret_state = [ret_state_ref[hd] for hd in range(N_HEADS)]
    ret_rows = []
    for c in range(tile // RET_CHUNK):
        heads = []
        for hd in range(N_HEADS):
            u, ret_state[hd] = _retention_chunk(hd, c, rope[hd][0], rope[hd][1], rv, rg, ret_state[hd],
                                                ret_g_ref, dm_ref, qd_ref, kd_ref, chunk_decay)
            heads.append(u)
        ret_rows.append(jnp.concatenate(heads, axis=1))
    for hd in range(N_HEADS):
        ret_state_ref[hd] = ret_state[hd]
    u_a = jnp.concatenate(ret_rows, axis=0)

    gq, gf, gi, gg = (seg(w_hgrn_ref, j) for j in range(4))
    hgrn_state = [hgrn_state_ref[hd] for hd in range(N_HEADS)]
    hgrn_rows = []
    for c in range(tile // HGRN_CHUNK):
        r0 = c * HGRN_CHUNK
        f_all, cum2_all = _hgrn_decay(gf[r0:r0 + HGRN_CHUNK], lbl_ref, tri_ref, layer)
        heads = []
        for hd in range(N_HEADS):
            u, hgrn_state[hd] = _hgrn_head(hd, r0, gq, gi, gg, f_all, cum2_all, hgrn_state[hd],
                                           hgrn_g_ref, lvl_ref)
            heads.append(u)
        hgrn_rows.append(jnp.concatenate(heads, axis=1))
    for hd in range(N_HEADS):
        hgrn_state_ref[hd] = hgrn_state[hd]
    u_b = jnp.concatenate(hgrn_rows, axis=0)

    y = None
    for i, (u, wg_ref) in enumerate(((u_a, wg0_ref), (u_b, wg1_ref), (u_c, wg2_ref))):
        gate = _sigmoid(_dot(h, wg_ref[...]))
        term = gate * _dot(u, wb_ref[i])
        y = term if y is None else y + term
    z = _dot(y.astype(BF16), wo_ref[...])
    o_ref[0] = _layer_norm(alpha * x + mod_ref[0, 2:3, :] * z, lg_ref[...], lb_ref[...])


def _token_mixer(x, mod, cos_t, sin_t, w_in, ret_norm_g, lb_logits, hgrn_norm_g,
                 conv_w, conv_b, conv_ln_g, conv_ln_b, w_branch, w_out, ln_g, ln_b, layer, alpha):
    bsz, seq, _ = x.shape
    tile = MIX_TILE
    assert tile == 2 * HGRN_CHUNK and seq % tile == 0
    depth = lb_logits.shape[0]
    dmask, qdec, kdec, cdec = _retention_constants()
    tri, lvl = _hgrn_constants()
    row = lambda b, i: (b, i, 0)
    vec = lambda a, n: a.reshape(1, n)
    w_seg = lambda width, idx: _const_spec((None, D_MODEL, width), (layer, 0, idx))
    return pl.pallas_call(
        functools.partial(_mixer_kernel, layer=layer, alpha=alpha, chunk_decay=cdec),
        grid=(bsz, seq // tile),
        in_specs=[
            pl.BlockSpec((1, tile, D_MODEL), row),
            pl.BlockSpec((1, N_MOD, D_MODEL), lambda b, i: (b, 0, 0)),
            pl.BlockSpec((1, tile, HEAD_DIM), row),
            pl.BlockSpec((1, tile, HEAD_DIM), row),
            w_seg(4 * WIDTH, 0),
            _const_spec((1, WIDTH)),
            _const_spec((N_HEADS, RET_CHUNK, RET_CHUNK)),
            _const_spec((N_HEADS, RET_CHUNK, HEAD_DIM)),
            _const_spec((N_HEADS, RET_CHUNK, HEAD_DIM)),
            w_seg(4 * WIDTH, 1),
            _const_spec((depth, WIDTH)),
            _const_spec((1, WIDTH)),
            _const_spec((HGRN_CHUNK, HGRN_CHUNK)),
            _const_spec((HGRN_BLOCK, HGRN_BLOCK)),
            w_seg(2 * WIDTH, 4),
            _const_spec((CONV_KERNEL, WIDTH)),
            _const_spec((1, WIDTH)),
            _const_spec((1, WIDTH)),
            _const_spec((1, WIDTH)),
            w_seg(D_MODEL, 5),
            w_seg(D_MODEL, 6),
            w_seg(D_MODEL, 7),
            _const_spec((N_BRANCH, WIDTH, D_MODEL)),
            _const_spec((D_MODEL, D_MODEL)),
            _const_spec((1, D_MODEL)),
            _const_spec((1, D_MODEL)),
        ],
        out_specs=pl.BlockSpec((1, tile, D_MODEL), row),
        out_shape=jax.ShapeDtypeStruct((bsz, seq, D_MODEL), F32),
        scratch_shapes=[
            pltpu.VMEM((N_HEADS, HEAD_DIM, HEAD_DIM), F32),
            pltpu.VMEM((N_HEADS, HEAD_DIM, HEAD_DIM), F32),
            pltpu.VMEM((CONV_HALO, WIDTH), F32),
        ],
        compiler_params=_params(("parallel", "arbitrary")),
        name="token_mixer",
    )(x, mod, cos_t, sin_t,
      w_in, vec(ret_norm_g, WIDTH), dmask, qdec, kdec,
      w_in, lb_logits.astype(F32), vec(hgrn_norm_g, WIDTH), tri, lvl,
      w_in, conv_w, vec(conv_b, WIDTH), vec(conv_ln_g, WIDTH), vec(conv_ln_b, WIDTH),
      w_in, w_in, w_in, w_branch, w_out, vec(ln_g, D_MODEL), vec(ln_b, D_MODEL))


def _ffn_kernel(x_ref, mod_ref, wu_ref, cw_ref, cb_ref, wd_ref, lg_ref, lb_ref,
                o_ref, hext_ref, act_ref, *, alpha):
    tile = x_ref.shape[1]

    @pl.when(pl.program_id(1) == 0)
    def _():
        hext_ref[0:FFN_HALO, :] = jnp.zeros((FFN_HALO, D_MODEL), BF16)

    x = x_ref[0]
    hext_ref[FFN_HALO:FFN_HALO + tile, :] = _modulate(x, mod_ref, 3).astype(BF16)
    hext = hext_ref[...]
    first = FFN_HALO - (FFN_CONV_KERNEL - 1)

    def conv(p, col):
        y = cb_ref[:, col:col + FFN_CHUNK]
        for j in range(FFN_CONV_KERNEL):
            y = y + p[first + j:first + j + tile] * cw_ref[j:j + 1, col:col + FFN_CHUNK]
        return y

    for c in range(D_FF // FFN_CHUNK):
        col_a = c * FFN_CHUNK
        col_v = D_FF + col_a
        pa = _dot(hext, wu_ref[:, col_a:col_a + FFN_CHUNK])
        pv = _dot(hext, wu_ref[:, col_v:col_v + FFN_CHUNK])
        act_ref[:, col_a:col_a + FFN_CHUNK] = (_silu(conv(pa, col_a)) * conv(pv, col_v)).astype(BF16)

    hext_ref[0:FFN_HALO, :] = hext_ref[tile:tile + FFN_HALO, :]
    res_gate = mod_ref[0, 5:6, :]
    for r in range(2):
        rows = slice(r * tile // 2, (r + 1) * tile // 2)
        y = _dot(act_ref[rows, :], wd_ref[...])
        o_ref[0, rows, :] = _layer_norm(alpha * x[rows] + res_gate * y, lg_ref[...], lb_ref[...])


def _conv_ffn(x, mod, w_up, conv_w, conv_b, w_down, ln_g, ln_b, alpha):
    bsz, seq, _ = x.shape
    tile = FFN_TILE
    row = lambda b, i: (b, i, 0)
    return pl.pallas_call(
        functools.partial(_ffn_kernel, alpha=alpha),
        grid=(bsz, seq // tile),
        in_specs=[
            pl.BlockSpec((1, tile, D_MODEL), row),
            pl.BlockSpec((1, N_MOD, D_MODEL), lambda b, i: (b, 0, 0)),
            _const_spec((D_MODEL, 2 * D_FF)),
            _const_spec((FFN_CONV_KERNEL, 2 * D_FF)),
            _const_spec((1, 2 * D_FF)),
            _const_spec((D_FF, D_MODEL)),
            _const_spec((1, D_MODEL)),
            _const_spec((1, D_MODEL)),
        ],
        out_specs=pl.BlockSpec((1, tile, D_MODEL), row),
        out_shape=jax.ShapeDtypeStruct((bsz, seq, D_MODEL), F32),
        scratch_shapes=[
            pltpu.VMEM((FFN_HALO + tile, D_MODEL), BF16),
            pltpu.VMEM((tile, D_FF), BF16),
        ],
        compiler_params=_params(("parallel", "arbitrary")),
        name="conv_ffn_ln",
    )(x, mod, w_up, conv_w, conv_b.reshape(1, 2 * D_FF), w_down,
      ln_g.reshape(1, D_MODEL), ln_b.reshape(1, D_MODEL))


def kernel(x, c, positions, w_ada, b_ada, w_in, ret_norm_g, hgrn_lb_logits, hgrn_norm_g, conv_w, conv_b,
           conv_ln_g, conv_ln_b, w_branch, w_out, ln1_g, ln1_b, ffn_w_up, ffn_conv_w, ffn_conv_b,
           ffn_w_down, ln2_g, ln2_b):
    depth = w_ada.shape[0]
    bsz = x.shape[0]
    alpha = (2.0 * depth) ** 0.25
    mod_all = _ada_mod(c, w_ada, b_ada)
    cos_t, sin_t = _rope_tables(positions)
    w_in_b = w_in.astype(BF16)
    for l in range(depth):
        mod = mod_all[l].reshape(bsz, N_MOD, D_MODEL)
        x = _token_mixer(x, mod, cos_t, sin_t, w_in_b, ret_norm_g[l], hgrn_lb_logits, hgrn_norm_g[l],
                         conv_w[l], conv_b[l], conv_ln_g[l], conv_ln_b[l],
                         w_branch[l].astype(BF16), w_out[l].astype(BF16), ln1_g[l], ln1_b[l], l, alpha)
        x = _conv_ffn(x, mod, ffn_w_up[l].astype(BF16), ffn_conv_w[l], ffn_conv_b[l],
                      ffn_w_down[l].astype(BF16), ln2_g[l], ln2_b[l], alpha)
    return x
```

```python
import functools

import numpy as np
import jax
import jax.numpy as jnp
from jax import lax
from jax.experimental import pallas as pl
from jax.experimental.pallas import tpu as pltpu

F32 = jnp.float32
BF16 = jnp.bfloat16

D_MODEL = 1024
N_HEADS = 4
HEAD_DIM = 128
WIDTH = N_HEADS * HEAD_DIM
RET_CHUNK = 256
ROPE_BASE = 10000.0
CONV_KERNEL = 31
N_BRANCH = 3
D_FF = 2816
FFN_CONV_KERNEL = 3
LN_EPS = 1e-5
N_MOD = 6
LOG2_E = 1.4426950408889634

V7X_LANES = 128
V7X_SUBLANES = 8
V7X_BF16_ROWS = 16
V7X_VMEM_LIMIT = 56 * 1024 * 1024

MIX_TILE = 512
HGRN_CHUNK = 256
HGRN_LEVELS = 8
HGRN_BLOCK = HGRN_CHUNK // 2
CONV_HALO = 32
FFN_TILE = 512
FFN_HALO = V7X_BF16_ROWS
FFN_CHUNK = 256
ADA_COLS = 1536


def _silu_half(xh):
    return xh + xh * jnp.tanh(xh)


def _gated_half(th, xh):
    return th + th * jnp.tanh(xh)


def _silu(x):
    return _silu_half(0.5 * x)


def _layer_norm(x, g, b):
    mu = jnp.mean(x, axis=-1, keepdims=True)
    xc = x - mu
    var = jnp.mean(xc * xc, axis=-1, keepdims=True)
    return xc * lax.rsqrt(var + LN_EPS) * g + b


def _dot(a, b):
    return jnp.dot(a, b, preferred_element_type=F32)


def _dot_nt(a, b):
    return lax.dot_general(a, b, (((1,), (1,)), ((), ())), preferred_element_type=F32)


def _dot_tn(a, b):
    return lax.dot_general(a, b, (((0,), (0,)), ((), ())), preferred_element_type=F32)


def _split3(x):
    hi = x.astype(BF16)
    r1 = x - hi.astype(F32)
    mid = r1.astype(BF16)
    lo = (r1 - mid.astype(F32)).astype(BF16)
    return hi, mid, lo


def _modulate(x, mod_ref, row):
    shift = mod_ref[0, row:row + 1, :]
    scale = mod_ref[0, row + 1:row + 2, :]
    return x * (1.0 + scale) + shift


def _params(semantics, flags=None):
    return pltpu.CompilerParams(dimension_semantics=semantics, vmem_limit_bytes=V7X_VMEM_LIMIT,
                                flags=flags)


def _const_spec(shape, index=None):
    index = (0,) * len(shape) if index is None else index
    return pl.BlockSpec(shape, lambda *_: index, pipeline_mode=pl.Buffered(1))


def _ada_kernel(c_ref, w_ref, b_ref, o_ref):
    cond = _silu(c_ref[...]).astype(BF16)
    o_ref[0] = _dot(cond, w_ref[0].astype(BF16)) + b_ref[0]


def _ada_mod(c, w_ada, b_ada):
    depth, _, n = w_ada.shape
    bsz = c.shape[0]
    return pl.pallas_call(
        _ada_kernel,
        grid=(depth, n // ADA_COLS),
        in_specs=[
            pl.BlockSpec((bsz, D_MODEL), lambda l, j: (0, 0)),
            pl.BlockSpec((1, D_MODEL, ADA_COLS), lambda l, j: (l, 0, j)),
            pl.BlockSpec((1, 1, ADA_COLS), lambda l, j: (l, 0, j)),
        ],
        out_specs=pl.BlockSpec((1, bsz, ADA_COLS), lambda l, j: (l, 0, j)),
        out_shape=jax.ShapeDtypeStruct((depth, bsz, n), F32),
        compiler_params=_params(("parallel", "parallel")),
        name="ada_mod",
    )(c, w_ada, b_ada.reshape(depth, 1, n))


def _rope_kernel(pos_ref, inv_ref, sign_ref, cos_ref, sin_ref):
    ang = pos_ref[0] * inv_ref[...]
    cos_ref[0] = jnp.cos(ang)
    sin_ref[0] = jnp.sin(ang) * sign_ref[...]


def _rope_tables(positions):
    bsz, seq = positions.shape
    half = HEAD_DIM // 2
    inv = ROPE_BASE ** (-jnp.arange(half, dtype=F32) / half)
    inv = jnp.concatenate([inv, inv]).reshape(1, HEAD_DIM)
    sign = jnp.concatenate([-jnp.ones((half,), F32), jnp.ones((half,), F32)]).reshape(1, HEAD_DIM)
    pos = jnp.broadcast_to(positions.astype(F32)[..., None], (bsz, seq, HEAD_DIM))
    tile = MIX_TILE
    spec = pl.BlockSpec((1, tile, HEAD_DIM), lambda b, i: (b, i, 0))
    return pl.pallas_call(
        _rope_kernel,
        grid=(bsz, seq // tile),
        in_specs=[spec, _const_spec((1, HEAD_DIM)), _const_spec((1, HEAD_DIM))],
        out_specs=[spec, spec],
        out_shape=[jax.ShapeDtypeStruct((bsz, seq, HEAD_DIM), F32)] * 2,
        compiler_params=_params(("parallel", "parallel")),
        name="rope_tables",
    )(pos, inv, sign)


def _retention_constants():
    c = RET_CHUNK
    log_gamma = np.log1p(-np.exp2(-5.0 - np.arange(N_HEADS, dtype=np.float64)))
    idx = np.arange(c, dtype=np.float64)
    rel = idx[:, None] - idx[None, :]
    mask = np.where(rel >= 0, np.exp(log_gamma[:, None, None] * np.maximum(rel, 0.0)), 0.0)
    qdec = np.exp(log_gamma[:, None] * (idx + 1.0))
    kdec = np.exp(log_gamma[:, None] * (c - 1.0 - idx))
    ones = np.ones((1, 1, HEAD_DIM))
    cdec = tuple(float(v) for v in np.exp(log_gamma * c))
    return (jnp.asarray(mask, F32), jnp.asarray(qdec[:, :, None] * ones, F32),
            jnp.asarray(kdec[:, :, None] * ones, F32), cdec)


def _retention_rope(hd, pq, pk, cos_ref, sin_ref):
    lo = hd * HEAD_DIM
    cosf = cos_ref[0]
    sinf = sin_ref[0]
    q = pq[:, lo:lo + HEAD_DIM]
    k = pk[:, lo:lo + HEAD_DIM]
    q = q * cosf + pltpu.roll(q, HEAD_DIM // 2, 1) * sinf
    k = (k * cosf + pltpu.roll(k, HEAD_DIM // 2, 1) * sinf) * (HEAD_DIM ** -0.5)
    return q, k


def _retention_chunk(hd, c, q, k, pv, pg, state, g_ref, dm_ref, qd_ref, kd_ref, chunk_decay):
    lo = hd * HEAD_DIM
    r0 = c * RET_CHUNK
    qc = q[r0:r0 + RET_CHUNK].astype(BF16)
    kf = k[r0:r0 + RET_CHUNK]
    kc = kf.astype(BF16)
    vc = pv[r0:r0 + RET_CHUNK, lo:lo + HEAD_DIM].astype(BF16)
    scores = _dot_nt(qc, kc) * dm_ref[hd]
    out = _dot(scores.astype(BF16), vc) + _dot(qc, state.astype(BF16)) * qd_ref[hd]
    new_state = chunk_decay[hd] * state + _dot_tn((kf * kd_ref[hd]).astype(BF16), vc)
    mu = jnp.mean(out, axis=-1, keepdims=True)
    oc = out - mu
    var = jnp.mean(oc * oc, axis=-1, keepdims=True)
    gate = _silu_half(pg[r0:r0 + RET_CHUNK, lo:lo + HEAD_DIM])
    u = oc * lax.rsqrt(var + LN_EPS) * g_ref[:, lo:lo + HEAD_DIM] * gate
    return u.astype(BF16), new_state


def _hgrn_constants():
    n = HGRN_CHUNK
    t = np.arange(n)
    tri = (t[:, None] >= t[None, :]).astype(np.float32)
    s = np.arange(HGRN_BLOCK)
    x = s[:, None] ^ s[None, :]
    level = np.where(s[:, None] > s[None, :], np.floor(np.log2(np.maximum(x, 1))), -1)
    return jnp.asarray(tri, BF16), jnp.asarray(level, jnp.int32)


def _hgrn_level_operand(q, k, f, cum2, level):
    n, d = q.shape
    group = 2 << level
    half = group // 2
    sub = V7X_SUBLANES
    if level == 0:
        odd = (lax.broadcasted_iota(jnp.int32, (n, d), 0) & 1) == 1
        return jnp.where(odd, q * f, k)
    if half >= sub:
        shape = (n // group, group, d)
        q3, k3, c3 = q.reshape(shape), k.reshape(shape), cum2.reshape(shape)
        pivot = c3[:, half - 1:half, :]
        lo = k3[:, :half, :] * jnp.exp2(pivot - c3[:, :half, :])
        hi = q3[:, half:, :] * jnp.exp2(c3[:, half:, :] - pivot)
        return jnp.concatenate([lo, hi], axis=1).reshape(n, d)
    shape = (n // sub, sub, d)
    q3, k3, c3 = q.reshape(shape), k.reshape(shape), cum2.reshape(shape)
    row = lax.broadcasted_iota(jnp.int32, shape, 1)
    if group == sub:
        pivot = c3[:, half - 1:half, :]
    else:
        pivot = jnp.where(row < 4, c3[:, 1:2, :], c3[:, 5:6, :])
    second = (row & half) != 0
    return (jnp.where(second, q3, k3) * jnp.exp2(-jnp.abs(c3 - pivot))).reshape(n, d)


def _hgrn_decay(gf, lbl_ref, tri_ref, layer):
    logits = lbl_ref[...]
    e = jnp.exp(logits - jnp.max(logits, axis=0, keepdims=True))
    sm = e / jnp.sum(e, axis=0, keepdims=True)
    lb = jnp.zeros((1, WIDTH), F32)
    for j in range(1, layer + 1):
        lb = lb + sm[j:j + 1, :]
    log_lb = jnp.log(lb)
    log_1m_lb = jnp.log(1.0 - lb)
    log_sig = jnp.minimum(gf, 0.0) - jnp.log(1.0 + jnp.exp(-jnp.abs(gf)))
    b = log_1m_lb + log_sig
    logf = jnp.maximum(log_lb, b) + jnp.log(1.0 + jnp.exp(-jnp.abs(log_lb - b)))
    tri = tri_ref[...]
    f_hi, f_mid, f_lo = _split3(logf * LOG2_E)
    cum2 = _dot(tri, f_hi) + _dot(tri, f_mid) + _dot(tri, f_lo)
    return jnp.exp(logf), cum2


def _hgrn_head(hd, r0, pq, pi, pg, f_all, cum2_all, state_t, g_ref, lvl_ref):
    lo = hd * HEAD_DIM
    blk = HGRN_BLOCK
    lvl = lvl_ref[...]
    q = _silu_half(pq[r0:r0 + HGRN_CHUNK, lo:lo + HEAD_DIM])
    v = pi[r0:r0 + HGRN_CHUNK, lo:lo + HEAD_DIM]
    vb = v.astype(BF16)
    gate = _silu_half(pg[r0:r0 + HGRN_CHUNK, lo:lo + HEAD_DIM])
    f = f_all[:, lo:lo + HEAD_DIM]
    k = 1.0 - f
    cum2 = cum2_all[:, lo:lo + HEAD_DIM]
    last2 = cum2[HGRN_CHUNK - 1:HGRN_CHUNK, :]

    diag = [None, None]
    for level in range(HGRN_LEVELS - 1):
        m = _hgrn_level_operand(q, k, f, cum2, level).astype(BF16)
        sel = lvl == level
        for bi in range(2):
            mb = m[bi * blk:(bi + 1) * blk]
            pr = _dot_nt(mb, mb)
            diag[bi] = jnp.where(sel, pr, 0.0 if diag[bi] is None else diag[bi])
    m = _hgrn_level_operand(q, k, f, cum2, HGRN_LEVELS - 1).astype(BF16)
    cross = _dot_nt(m[blk:], m[:blk])

    out_lo = _dot(diag[0].astype(BF16), vb[:blk])
    out_hi = _dot(jnp.concatenate([cross, diag[1]], axis=1).astype(BF16), vb)
    out = jnp.concatenate([out_lo, out_hi], axis=0)
    out = out + jnp.sum(q * k, axis=-1, keepdims=True) * v
    qe = (q * jnp.exp2(cum2)).astype(BF16)
    out = out + _dot_nt(qe, state_t.astype(BF16))
    ke = (k * jnp.exp2(last2 - cum2)).astype(BF16)
    new_state = state_t * jnp.exp2(last2) + _dot_tn(vb, ke)
    ms = jnp.mean(out * out, axis=-1, keepdims=True)
    u = out * lax.rsqrt(ms + LN_EPS) * g_ref[:, lo:lo + HEAD_DIM] * gate
    return u.astype(BF16), new_state


def _causal_conv31(ext, tile, cw_ref, cb_ref):
    first = CONV_HALO - (CONV_KERNEL - 1)
    sub = V7X_SUBLANES
    acc = None
    for s in range(sub):
        offs = [o for o in range(first, first + CONV_KERNEL) if o % sub == s]
        span = tile + (sub if s else 0)
        z = None
        for o in offs:
            term = ext[o - s:o - s + span] * cw_ref[o - first:o - first + 1, :]
            z = term if z is None else z + term
        z = z[s:s + tile]
        acc = z + cb_ref[...] if acc is None else acc + z
    return acc


def _mixer_kernel(x_ref, mod_ref, cos_ref, sin_ref,
                  w_ret_ref, ret_g_ref, dm_ref, qd_ref, kd_ref,
                  w_hgrn_ref, lbl_ref, hgrn_g_ref, tri_ref, lvl_ref,
                  w_conv_ref, cw_ref, cb_ref, clg_ref, clb_ref,
                  wg0_ref, wg1_ref, wg2_ref, wb_ref, wo_ref, lg_ref, lb_ref,
                  o_ref,
                  ret_state_ref, hgrn_state_ref, conv_hist_ref,
                  *, layer, alpha, chunk_decay):
    @pl.when(pl.program_id(1) == 0)
    def _():
        ret_state_ref[...] = jnp.zeros_like(ret_state_ref)
        hgrn_state_ref[...] = jnp.zeros_like(hgrn_state_ref)
        conv_hist_ref[...] = jnp.zeros_like(conv_hist_ref)

    tile = x_ref.shape[1]
    x = x_ref[0]
    h = _modulate(x, mod_ref, 0).astype(BF16)
    seg = lambda w_ref, j: _dot(h, w_ref[:, j * WIDTH:(j + 1) * WIDTH])

    glu = _gated_half(seg(w_conv_ref, 0), seg(w_conv_ref, 1))
    ext = jnp.concatenate([conv_hist_ref[...], glu], axis=0)
    conv_hist_ref[...] = glu[tile - CONV_HALO:]
    conv = _causal_conv31(ext, tile, cw_ref, cb_ref)
    u_c = _silu(_layer_norm(conv, clg_ref[...], clb_ref[...])).astype(BF16)

    rq, rk, rv, rg = (seg(w_ret_ref, j) for j in range(4))
    rope = [_retention_rope(hd, rq, rk, cos_ref, sin_ref) for hd in range(N_HEADS)]
    ret_state = [ret_state_ref[hd] for hd in range(N_HEADS)]
    ret_rows = []
    for c in range(tile // RET_CHUNK):
        heads = []
        for hd in range(N_HEADS):
            u, ret_state[hd] = _retention_chunk(hd, c, rope[hd][0], rope[hd][1], rv, rg, ret_state[hd],
                                                ret_g_ref, dm_ref, qd_ref, kd_ref, chunk_decay)
            heads.append(u)
        ret_rows.append(jnp.concatenate(heads, axis=1))
    for hd in range(N_HEADS):
        ret_state_ref[hd] = ret_state[hd]
    u_a = jnp.concatenate(ret_rows, axis=0)

    gq, gf, gi, gg = (seg(w_hgrn_ref, j) for j in range(4))
    hgrn_state = [hgrn_state_ref[hd] for hd in range(N_HEADS)]
    hgrn_rows = []
    for c in range(tile // HGRN_CHUNK):
        r0 = c * HGRN_CHUNK
        f_all, cum2_all = _hgrn_decay(gf[r0:r0 + HGRN_CHUNK], lbl_ref, tri_ref, layer)
        heads = []
        for hd in range(N_HEADS):
            u, hgrn_state[hd] = _hgrn_head(hd, r0, gq, gi, gg, f_all, cum2_all, hgrn_state[hd],
                                           hgrn_g_ref, lvl_ref)
            heads.append(u)
        hgrn_rows.append(jnp.concatenate(heads, axis=1))
    for hd in range(N_HEADS):
        hgrn_state_ref[hd] = hgrn_state[hd]
    u_b = jnp.concatenate(hgrn_rows, axis=0)

    y = None
    for i, (u, wg_ref) in enumerate(((u_a, wg0_ref), (u_b, wg1_ref), (u_c, wg2_ref))):
        term = _gated_half(_dot(u, wb_ref[i]), _dot(h, wg_ref[...]))
        y = term if y is None else y + term
    z = _dot(y.astype(BF16), wo_ref[...])
    o_ref[0] = _layer_norm(alpha * x + mod_ref[0, 2:3, :] * z, lg_ref[...], lb_ref[...])


def _token_mixer(x, mod, cos_t, sin_t, w_in, ret_norm_g, lb_logits, hgrn_norm_g,
                 conv_w, conv_b, conv_ln_g, conv_ln_b, w_branch, w_out, ln_g, ln_b, layer, alpha):
    bsz, seq, _ = x.shape
    tile = MIX_TILE
    assert tile == 2 * HGRN_CHUNK and seq % tile == 0
    depth = lb_logits.shape[0]
    dmask, qdec, kdec, cdec = _retention_constants()
    tri, lvl = _hgrn_constants()
    row = lambda b, i: (b, i, 0)
    vec = lambda a, n: a.reshape(1, n)
    w_seg = lambda width, idx: _const_spec((None, D_MODEL, width), (layer, 0, idx))
    return pl.pallas_call(
        functools.partial(_mixer_kernel, layer=layer, alpha=alpha, chunk_decay=cdec),
        grid=(bsz, seq // tile),
        in_specs=[
            pl.BlockSpec((1, tile, D_MODEL), row),
            pl.BlockSpec((1, N_MOD, D_MODEL), lambda b, i: (b, 0, 0)),
            pl.BlockSpec((1, tile, HEAD_DIM), row),
            pl.BlockSpec((1, tile, HEAD_DIM), row),
            w_seg(4 * WIDTH, 0),
            _const_spec((1, WIDTH)),
            _const_spec((N_HEADS, RET_CHUNK, RET_CHUNK)),
            _const_spec((N_HEADS, RET_CHUNK, HEAD_DIM)),
            _const_spec((N_HEADS, RET_CHUNK, HEAD_DIM)),
            w_seg(4 * WIDTH, 1),
            _const_spec((depth, WIDTH)),
            _const_spec((1, WIDTH)),
            _const_spec((HGRN_CHUNK, HGRN_CHUNK)),
            _const_spec((HGRN_BLOCK, HGRN_BLOCK)),
            w_seg(2 * WIDTH, 4),
            _const_spec((CONV_KERNEL, WIDTH)),
            _const_spec((1, WIDTH)),
            _const_spec((1, WIDTH)),
            _const_spec((1, WIDTH)),
            w_seg(D_MODEL, 5),
            w_seg(D_MODEL, 6),
            w_seg(D_MODEL, 7),
            _const_spec((N_BRANCH, WIDTH, D_MODEL)),
            _const_spec((D_MODEL, D_MODEL)),
            _const_spec((1, D_MODEL)),
            _const_spec((1, D_MODEL)),
        ],
        out_specs=pl.BlockSpec((1, tile, D_MODEL), row),
        out_shape=jax.ShapeDtypeStruct((bsz, seq, D_MODEL), F32),
        scratch_shapes=[
            pltpu.VMEM((N_HEADS, HEAD_DIM, HEAD_DIM), F32),
            pltpu.VMEM((N_HEADS, HEAD_DIM, HEAD_DIM), F32),
            pltpu.VMEM((CONV_HALO, WIDTH), F32),
        ],
        compiler_params=_params(("parallel", "arbitrary")),
        name="token_mixer",
    )(x, mod, cos_t, sin_t,
      w_in, vec(ret_norm_g, WIDTH), dmask, qdec, kdec,
      w_in, lb_logits.astype(F32), vec(hgrn_norm_g, WIDTH), tri, lvl,
      w_in, conv_w, vec(conv_b, WIDTH), vec(conv_ln_g, WIDTH), vec(conv_ln_b, WIDTH),
      w_in, w_in, w_in, w_branch, w_out, vec(ln_g, D_MODEL), vec(ln_b, D_MODEL))


def _ffn_kernel(x_ref, mod_ref, wu_ref, cw_ref, cb_ref, wd_ref, lg_ref, lb_ref,
                o_ref, hext_ref, act_ref, *, alpha):
    tile = x_ref.shape[1]

    @pl.when(pl.program_id(1) == 0)
    def _():
        hext_ref[0:FFN_HALO, :] = jnp.zeros((FFN_HALO, D_MODEL), BF16)

    x = x_ref[0]
    hext_ref[FFN_HALO:FFN_HALO + tile, :] = _modulate(x, mod_ref, 3).astype(BF16)
    hext = hext_ref[...]
    first = FFN_HALO - (FFN_CONV_KERNEL - 1)

    def conv(p, col):
        y = cb_ref[:, col:col + FFN_CHUNK]
        for j in range(FFN_CONV_KERNEL):
            y = y + p[first + j:first + j + tile] * cw_ref[j:j + 1, col:col + FFN_CHUNK]
        return y

    for c in range(D_FF // FFN_CHUNK):
        col_a = c * FFN_CHUNK
        col_v = D_FF + col_a
        pa = _dot(hext, wu_ref[:, col_a:col_a + FFN_CHUNK])
        pv = _dot(hext, wu_ref[:, col_v:col_v + FFN_CHUNK])
        act_ref[:, col_a:col_a + FFN_CHUNK] = (_silu_half(conv(pa, col_a)) * conv(pv, col_v)).astype(BF16)

    hext_ref[0:FFN_HALO, :] = hext_ref[tile:tile + FFN_HALO, :]
    res_gate = mod_ref[0, 5:6, :]
    for r in range(2):
        rows = slice(r * tile // 2, (r + 1) * tile // 2)
        y = _dot(act_ref[rows, :], wd_ref[...])
        o_ref[0, rows, :] = _layer_norm(alpha * x[rows] + res_gate * y, lg_ref[...], lb_ref[...])


def _conv_ffn(x, mod, w_up, conv_w, conv_b, w_down, ln_g, ln_b, alpha):
    bsz, seq, _ = x.shape
    tile = FFN_TILE
    row = lambda b, i: (b, i, 0)
    return pl.pallas_call(
        functools.partial(_ffn_kernel, alpha=alpha),
        grid=(bsz, seq // tile),
        in_specs=[
            pl.BlockSpec((1, tile, D_MODEL), row),
            pl.BlockSpec((1, N_MOD, D_MODEL), lambda b, i: (b, 0, 0)),
            _const_spec((D_MODEL, 2 * D_FF)),
            _const_spec((FFN_CONV_KERNEL, 2 * D_FF)),
            _const_spec((1, 2 * D_FF)),
            _const_spec((D_FF, D_MODEL)),
            _const_spec((1, D_MODEL)),
            _const_spec((1, D_MODEL)),
        ],
        out_specs=pl.BlockSpec((1, tile, D_MODEL), row),
        out_shape=jax.ShapeDtypeStruct((bsz, seq, D_MODEL), F32),
        scratch_shapes=[
            pltpu.VMEM((FFN_HALO + tile, D_MODEL), BF16),
            pltpu.VMEM((tile, D_FF), BF16),
        ],
        compiler_params=_params(("parallel", "arbitrary")),
        name="conv_ffn_ln",
    )(x, mod, w_up, conv_w, conv_b.reshape(1, 2 * D_FF), w_down,
      ln_g.reshape(1, D_MODEL), ln_b.reshape(1, D_MODEL))


def kernel(x, c, positions, w_ada, b_ada, w_in, ret_norm_g, hgrn_lb_logits, hgrn_norm_g, conv_w, conv_b,
           conv_ln_g, conv_ln_b, w_branch, w_out, ln1_g, ln1_b, ffn_w_up, ffn_conv_w, ffn_conv_b,
           ffn_w_down, ln2_g, ln2_b):
    depth = w_ada.shape[0]
    bsz = x.shape[0]
    alpha = (2.0 * depth) ** 0.25
    mod_all = _ada_mod(c, w_ada, b_ada)
    cos_t, sin_t = _rope_tables(positions)
    in_scale = np.ones((w_in.shape[-1],), np.float32)
    for a, b in ((3, 4), (4, 5), (7, 8), (8, 10), (10, 16)):
        in_scale[a * WIDTH:b * WIDTH] = 0.5
    w_in_b = (w_in * in_scale).astype(BF16)
    ffn_scale = np.concatenate([np.full((D_FF,), 0.5, np.float32), np.ones((D_FF,), np.float32)])
    for l in range(depth):
        mod = mod_all[l].reshape(bsz, N_MOD, D_MODEL)
        x = _token_mixer(x, mod, cos_t, sin_t, w_in_b, ret_norm_g[l], hgrn_lb_logits, hgrn_norm_g[l],
                         conv_w[l], conv_b[l], conv_ln_g[l], conv_ln_b[l],
                         (0.5 * w_branch[l]).astype(BF16), w_out[l].astype(BF16), ln1_g[l], ln1_b[l], l, alpha)
        x = _conv_ffn(x, mod, ffn_w_up[l].astype(BF16), ffn_conv_w[l] * ffn_scale, ffn_conv_b[l] * ffn_scale,
                      ffn_w_down[l].astype(BF16), ln2_g[l], ln2_b[l], alpha)
    return x
```

```python
import functools

import numpy as np
import jax
import jax.numpy as jnp
from jax import lax
from jax.experimental import pallas as pl
from jax.experimental.pallas import tpu as pltpu

F32 = jnp.float32
BF16 = jnp.bfloat16

D_MODEL = 1024
N_HEADS = 4
HEAD_DIM = 128
WIDTH = N_HEADS * HEAD_DIM
RET_CHUNK = 256
ROPE_BASE = 10000.0
CONV_KERNEL = 31
N_BRANCH = 3
D_FF = 2816
FFN_CONV_KERNEL = 3
LN_EPS = 1e-5
N_MOD = 6
LOG2_E = 1.4426950408889634

V7X_LANES = 128
V7X_SUBLANES = 8
V7X_BF16_ROWS = 16
V7X_VMEM_LIMIT = 56 * 1024 * 1024

MIX_TILE = 512
HGRN_CHUNK = 256
HGRN_LEVELS = 8
HGRN_BLOCK = HGRN_CHUNK // 2
CONV_HALO = 32
FFN_TILE = 512
FFN_HALO = V7X_BF16_ROWS
FFN_CHUNK = 256
FFN_OUT_BLOCKS = 4
ADA_COLS = 1536


def _silu_half(xh):
    return xh + xh * jnp.tanh(xh)


def _gated_half(th, xh):
    return th + th * jnp.tanh(xh)


def _silu(x):
    return _silu_half(0.5 * x)


def _layer_norm(x, g, b):
    mu = jnp.mean(x, axis=-1, keepdims=True)
    xc = x - mu
    var = jnp.mean(xc * xc, axis=-1, keepdims=True)
    return xc * lax.rsqrt(var + LN_EPS) * g + b


def _dot(a, b):
    return jnp.dot(a, b, preferred_element_type=F32)


def _dot_nt(a, b):
    return lax.dot_general(a, b, (((1,), (1,)), ((), ())), preferred_element_type=F32)


def _dot_tn(a, b):
    return lax.dot_general(a, b, (((0,), (0,)), ((), ())), preferred_element_type=F32)


def _split3(x):
    hi = x.astype(BF16)
    r1 = x - hi.astype(F32)
    mid = r1.astype(BF16)
    lo = (r1 - mid.astype(F32)).astype(BF16)
    return hi, mid, lo


def _modulate(x, mod_ref, row):
    shift = mod_ref[0, row:row + 1, :]
    scale = mod_ref[0, row + 1:row + 2, :]
    return x * (1.0 + scale) + shift


def _params(semantics, flags=None):
    return pltpu.CompilerParams(dimension_semantics=semantics, vmem_limit_bytes=V7X_VMEM_LIMIT,
                                flags=flags)


def _const_spec(shape, index=None):
    index = (0,) * len(shape) if index is None else index
    return pl.BlockSpec(shape, lambda *_: index, pipeline_mode=pl.Buffered(1))


def _ada_kernel(c_ref, w_ref, b_ref, o_ref):
    cond = _silu(c_ref[...]).astype(BF16)
    o_ref[0] = _dot(cond, w_ref[0].astype(BF16)) + b_ref[0]


def _ada_mod(c, w_ada, b_ada):
    depth, _, n = w_ada.shape
    bsz = c.shape[0]
    return pl.pallas_call(
        _ada_kernel,
        grid=(depth, n // ADA_COLS),
        in_specs=[
            pl.BlockSpec((bsz, D_MODEL), lambda l, j: (0, 0)),
            pl.BlockSpec((1, D_MODEL, ADA_COLS), lambda l, j: (l, 0, j)),
            pl.BlockSpec((1, 1, ADA_COLS), lambda l, j: (l, 0, j)),
        ],
        out_specs=pl.BlockSpec((1, bsz, ADA_COLS), lambda l, j: (l, 0, j)),
        out_shape=jax.ShapeDtypeStruct((depth, bsz, n), F32),
        compiler_params=_params(("parallel", "parallel")),
        name="ada_mod",
    )(c, w_ada, b_ada.reshape(depth, 1, n))


def _rope_kernel(pos_ref, inv_ref, sign_ref, cos_ref, sin_ref):
    ang = pos_ref[0] * inv_ref[...]
    cos_ref[0] = jnp.cos(ang)
    sin_ref[0] = jnp.sin(ang) * sign_ref[...]


def _rope_tables(positions):
    bsz, seq = positions.shape
    half = HEAD_DIM // 2
    inv = ROPE_BASE ** (-jnp.arange(half, dtype=F32) / half)
    inv = jnp.concatenate([inv, inv]).reshape(1, HEAD_DIM)
    sign = jnp.concatenate([-jnp.ones((half,), F32), jnp.ones((half,), F32)]).reshape(1, HEAD_DIM)
    pos = jnp.broadcast_to(positions.astype(F32)[..., None], (bsz, seq, HEAD_DIM))
    tile = MIX_TILE
    spec = pl.BlockSpec((1, tile, HEAD_DIM), lambda b, i: (b, i, 0))
    return pl.pallas_call(
        _rope_kernel,
        grid=(bsz, seq // tile),
        in_specs=[spec, _const_spec((1, HEAD_DIM)), _const_spec((1, HEAD_DIM))],
        out_specs=[spec, spec],
        out_shape=[jax.ShapeDtypeStruct((bsz, seq, HEAD_DIM), F32)] * 2,
        compiler_params=_params(("parallel", "parallel")),
        name="rope_tables",
    )(pos, inv, sign)


def _retention_constants():
    c = RET_CHUNK
    log_gamma = np.log1p(-np.exp2(-5.0 - np.arange(N_HEADS, dtype=np.float64)))
    idx = np.arange(c, dtype=np.float64)
    rel = idx[:, None] - idx[None, :]
    mask = np.where(rel >= 0, np.exp(log_gamma[:, None, None] * np.maximum(rel, 0.0)), 0.0)
    qdec = np.exp(log_gamma[:, None] * (idx + 1.0))
    kdec = np.exp(log_gamma[:, None] * (c - 1.0 - idx))
    ones = np.ones((1, 1, HEAD_DIM))
    cdec = tuple(float(v) for v in np.exp(log_gamma * c))
    return (jnp.asarray(mask, F32), jnp.asarray(qdec[:, :, None] * ones, F32),
            jnp.asarray(kdec[:, :, None] * ones, F32), cdec)


def _retention_rope(hd, pq, pk, cos_ref, sin_ref):
    lo = hd * HEAD_DIM
    cosf = cos_ref[0]
    sinf = sin_ref[0]
    q = pq[:, lo:lo + HEAD_DIM]
    k = pk[:, lo:lo + HEAD_DIM]
    q = q * cosf + pltpu.roll(q, HEAD_DIM // 2, 1) * sinf
    k = (k * cosf + pltpu.roll(k, HEAD_DIM // 2, 1) * sinf) * (HEAD_DIM ** -0.5)
    return q, k


def _retention_chunk(hd, c, q, k, pv, pg, state, g_ref, dm_ref, qd_ref, kd_ref, chunk_decay):
    lo = hd * HEAD_DIM
    r0 = c * RET_CHUNK
    qc = q[r0:r0 + RET_CHUNK].astype(BF16)
    kf = k[r0:r0 + RET_CHUNK]
    kc = kf.astype(BF16)
    vc = pv[r0:r0 + RET_CHUNK, lo:lo + HEAD_DIM].astype(BF16)
    scores = _dot_nt(qc, kc) * dm_ref[hd]
    out = _dot(scores.astype(BF16), vc) + _dot(qc, state.astype(BF16)) * qd_ref[hd]
    new_state = chunk_decay[hd] * state + _dot_tn((kf * kd_ref[hd]).astype(BF16), vc)
    mu = jnp.mean(out, axis=-1, keepdims=True)
    oc = out - mu
    var = jnp.mean(oc * oc, axis=-1, keepdims=True)
    gate = _silu_half(pg[r0:r0 + RET_CHUNK, lo:lo + HEAD_DIM])
    u = oc * lax.rsqrt(var + LN_EPS) * g_ref[:, lo:lo + HEAD_DIM] * gate
    return u.astype(BF16), new_state


def _hgrn_constants():
    n = HGRN_CHUNK
    t = np.arange(n)
    tri = (t[:, None] >= t[None, :]).astype(np.float32)
    s = np.arange(HGRN_BLOCK)
    x = s[:, None] ^ s[None, :]
    level = np.where(s[:, None] > s[None, :], np.floor(np.log2(np.maximum(x, 1))), -1)
    return jnp.asarray(tri, BF16), jnp.asarray(level, jnp.int32)


def _hgrn_level_operand(q, k, f, cum2, level):
    n, d = q.shape
    group = 2 << level
    half = group // 2
    sub = V7X_SUBLANES
    if level == 0:
        odd = (lax.broadcasted_iota(jnp.int32, (n, d), 0) & 1) == 1
        return jnp.where(odd, q * f, k)
    if half >= sub:
        shape = (n // group, group, d)
        q3, k3, c3 = q.reshape(shape), k.reshape(shape), cum2.reshape(shape)
        pivot = c3[:, half - 1:half, :]
        lo = k3[:, :half, :] * jnp.exp2(pivot - c3[:, :half, :])
        hi = q3[:, half:, :] * jnp.exp2(c3[:, half:, :] - pivot)
        return jnp.concatenate([lo, hi], axis=1).reshape(n, d)
    shape = (n // sub, sub, d)
    q3, k3, c3 = q.reshape(shape), k.reshape(shape), cum2.reshape(shape)
    row = lax.broadcasted_iota(jnp.int32, shape, 1)
    if group == sub:
        pivot = c3[:, half - 1:half, :]
    else:
        pivot = jnp.where(row < 4, c3[:, 1:2, :], c3[:, 5:6, :])
    second = (row & half) != 0
    return (jnp.where(second, q3, k3) * jnp.exp2(-jnp.abs(c3 - pivot))).reshape(n, d)


def _hgrn_decay(gf, lbl_ref, tri_ref, layer):
    logits = lbl_ref[...]
    e = jnp.exp(logits - jnp.max(logits, axis=0, keepdims=True))
    sm = e / jnp.sum(e, axis=0, keepdims=True)
    lb = jnp.zeros((1, WIDTH), F32)
    for j in range(1, layer + 1):
        lb = lb + sm[j:j + 1, :]
    log_lb = jnp.log(lb)
    log_1m_lb = jnp.log(1.0 - lb)
    log_sig = jnp.minimum(gf, 0.0) - jnp.log(1.0 + jnp.exp(-jnp.abs(gf)))
    b = log_1m_lb + log_sig
    logf = jnp.maximum(log_lb, b) + jnp.log(1.0 + jnp.exp(-jnp.abs(log_lb - b)))
    tri = tri_ref[...]
    f_hi, f_mid, f_lo = _split3(logf * LOG2_E)
    cum2 = _dot(tri, f_hi) + _dot(tri, f_mid) + _dot(tri, f_lo)
    return jnp.exp(logf), cum2


def _hgrn_head(hd, r0, pq, pi, pg, f_all, cum2_all, state_t, g_ref, lvl_ref):
    lo = hd * HEAD_DIM
    blk = HGRN_BLOCK
    lvl = lvl_ref[...]
    q = _silu_half(pq[r0:r0 + HGRN_CHUNK, lo:lo + HEAD_DIM])
    v = pi[r0:r0 + HGRN_CHUNK, lo:lo + HEAD_DIM]
    vb = v.astype(BF16)
    gate = _silu_half(pg[r0:r0 + HGRN_CHUNK, lo:lo + HEAD_DIM])
    f = f_all[:, lo:lo + HEAD_DIM]
    k = 1.0 - f
    cum2 = cum2_all[:, lo:lo + HEAD_DIM]
    last2 = cum2[HGRN_CHUNK - 1:HGRN_CHUNK, :]

    diag = [None, None]
    for level in range(HGRN_LEVELS - 1):
        m = _hgrn_level_operand(q, k, f, cum2, level).astype(BF16)
        sel = lvl == level
        for bi in range(2):
            mb = m[bi * blk:(bi + 1) * blk]
            pr = _dot_nt(mb, mb)
            diag[bi] = jnp.where(sel, pr, 0.0 if diag[bi] is None else diag[bi])
    m = _hgrn_level_operand(q, k, f, cum2, HGRN_LEVELS - 1).astype(BF16)
    cross = _dot_nt(m[blk:], m[:blk])

    out_lo = _dot(diag[0].astype(BF16), vb[:blk])
    out_hi = _dot(jnp.concatenate([cross, diag[1]], axis=1).astype(BF16), vb)
    out = jnp.concatenate([out_lo, out_hi], axis=0)
    out = out + jnp.sum(q * k, axis=-1, keepdims=True) * v
    qe = (q * jnp.exp2(cum2)).astype(BF16)
    out = out + _dot_nt(qe, state_t.astype(BF16))
    ke = (k * jnp.exp2(last2 - cum2)).astype(BF16)
    new_state = state_t * jnp.exp2(last2) + _dot_tn(vb, ke)
    ms = jnp.mean(out * out, axis=-1, keepdims=True)
    u = out * lax.rsqrt(ms + LN_EPS) * g_ref[:, lo:lo + HEAD_DIM] * gate
    return u.astype(BF16), new_state


def _causal_conv31(ext, tile, cw_ref, cb_ref):
    first = CONV_HALO - (CONV_KERNEL - 1)
    sub = V7X_SUBLANES
    acc = None
    for s in range(sub):
        offs = [o for o in range(first, first + CONV_KERNEL) if o % sub == s]
        span = tile + (sub if s else 0)
        z = None
        for o in offs:
            term = ext[o - s:o - s + span] * cw_ref[o - first:o - first + 1, :]
            z = term if z is None else z + term
        z = z[s:s + tile]
        acc = z + cb_ref[...] if acc is None else acc + z
    return acc


def _mixer_kernel(x_ref, mod_ref, cos_ref, sin_ref,
                  w_ret_ref, ret_g_ref, dm_ref, qd_ref, kd_ref,
                  w_hgrn_ref, lbl_ref, hgrn_g_ref, tri_ref, lvl_ref,
                  w_conv_ref, cw_ref, cb_ref, clg_ref, clb_ref,
                  wg0_ref, wg1_ref, wg2_ref, wb_ref, wo_ref, lg_ref, lb_ref,
                  o_ref,
                  ret_state_ref, hgrn_state_ref, conv_hist_ref,
                  *, layer, alpha, chunk_decay):
    @pl.when(pl.program_id(1) == 0)
    def _():
        ret_state_ref[...] = jnp.zeros_like(ret_state_ref)
        hgrn_state_ref[...] = jnp.zeros_like(hgrn_state_ref)
        conv_hist_ref[...] = jnp.zeros_like(conv_hist_ref)

    tile = x_ref.shape[1]
    x = x_ref[0]
    h = _modulate(x, mod_ref, 0).astype(BF16)
    seg = lambda w_ref, j: _dot(h, w_ref[:, j * WIDTH:(j + 1) * WIDTH])

    glu = _gated_half(seg(w_conv_ref, 0), seg(w_conv_ref, 1))
    ext = jnp.concatenate([conv_hist_ref[...], glu], axis=0)
    conv_hist_ref[...] = glu[tile - CONV_HALO:]
    conv = _causal_conv31(ext, tile, cw_ref, cb_ref)
    u_c = _silu(_layer_norm(conv, clg_ref[...], clb_ref[...])).astype(BF16)

    rq, rk, rv, rg = (seg(w_ret_ref, j) for j in range(4))
    rope = [_retention_rope(hd, rq, rk, cos_ref, sin_ref) for hd in range(N_HEADS)]
    ret_state = [ret_state_ref[hd] for hd in range(N_HEADS)]
    ret_rows = []
    for c in range(tile // RET_CHUNK):
        heads = []
        for hd in range(N_HEADS):
            u, ret_state[hd] = _retention_chunk(hd, c, rope[hd][0], rope[hd][1], rv, rg, ret_state[hd],
                                                ret_g_ref, dm_ref, qd_ref, kd_ref, chunk_decay)
            heads.append(u)
        ret_rows.append(jnp.concatenate(heads, axis=1))
    for hd in range(N_HEADS):
        ret_state_ref[hd] = ret_state[hd]
    u_a = jnp.concatenate(ret_rows, axis=0)

    gq, gf, gi, gg = (seg(w_hgrn_ref, j) for j in range(4))
    hgrn_state = [hgrn_state_ref[hd] for hd in range(N_HEADS)]
    hgrn_rows = []
    for c in range(tile // HGRN_CHUNK):
        r0 = c * HGRN_CHUNK
        f_all, cum2_all = _hgrn_decay(gf[r0:r0 + HGRN_CHUNK], lbl_ref, tri_ref, layer)
        heads = []
        for hd in range(N_HEADS):
            u, hgrn_state[hd] = _hgrn_head(hd, r0, gq, gi, gg, f_all, cum2_all, hgrn_state[hd],
                                           hgrn_g_ref, lvl_ref)
            heads.append(u)
        hgrn_rows.append(jnp.concatenate(heads, axis=1))
    for hd in range(N_HEADS):
        hgrn_state_ref[hd] = hgrn_state[hd]
    u_b = jnp.concatenate(hgrn_rows, axis=0)

    y = None
    for i, (u, wg_ref) in enumerate(((u_a, wg0_ref), (u_b, wg1_ref), (u_c, wg2_ref))):
        term = _gated_half(_dot(u, wb_ref[i]), _dot(h, wg_ref[...]))
        y = term if y is None else y + term
    z = _dot(y.astype(BF16), wo_ref[...])
    o_ref[0] = _layer_norm(alpha * x + mod_ref[0, 2:3, :] * z, lg_ref[...], lb_ref[...])


def _token_mixer(x, mod, cos_t, sin_t, w_in, ret_norm_g, lb_logits, hgrn_norm_g,
                 conv_w, conv_b, conv_ln_g, conv_ln_b, w_branch, w_out, ln_g, ln_b, layer, alpha):
    bsz, seq, _ = x.shape
    tile = MIX_TILE
    assert tile == 2 * HGRN_CHUNK and seq % tile == 0
    depth = lb_logits.shape[0]
    dmask, qdec, kdec, cdec = _retention_constants()
    tri, lvl = _hgrn_constants()
    row = lambda b, i: (b, i, 0)
    vec = lambda a, n: a.reshape(1, n)
    w_seg = lambda width, idx: _const_spec((None, D_MODEL, width), (layer, 0, idx))
    return pl.pallas_call(
        functools.partial(_mixer_kernel, layer=layer, alpha=alpha, chunk_decay=cdec),
        grid=(bsz, seq // tile),
        in_specs=[
            pl.BlockSpec((1, tile, D_MODEL), row),
            pl.BlockSpec((1, N_MOD, D_MODEL), lambda b, i: (b, 0, 0)),
            pl.BlockSpec((1, tile, HEAD_DIM), row),
            pl.BlockSpec((1, tile, HEAD_DIM), row),
            w_seg(4 * WIDTH, 0),
            _const_spec((1, WIDTH)),
            _const_spec((N_HEADS, RET_CHUNK, RET_CHUNK)),
            _const_spec((N_HEADS, RET_CHUNK, HEAD_DIM)),
            _const_spec((N_HEADS, RET_CHUNK, HEAD_DIM)),
            w_seg(4 * WIDTH, 1),
            _const_spec((depth, WIDTH)),
            _const_spec((1, WIDTH)),
            _const_spec((HGRN_CHUNK, HGRN_CHUNK)),
            _const_spec((HGRN_BLOCK, HGRN_BLOCK)),
            w_seg(2 * WIDTH, 4),
            _const_spec((CONV_KERNEL, WIDTH)),
            _const_spec((1, WIDTH)),
            _const_spec((1, WIDTH)),
            _const_spec((1, WIDTH)),
            w_seg(D_MODEL, 5),
            w_seg(D_MODEL, 6),
            w_seg(D_MODEL, 7),
            _const_spec((N_BRANCH, WIDTH, D_MODEL)),
            _const_spec((D_MODEL, D_MODEL)),
            _const_spec((1, D_MODEL)),
            _const_spec((1, D_MODEL)),
        ],
        out_specs=pl.BlockSpec((1, tile, D_MODEL), row),
        out_shape=jax.ShapeDtypeStruct((bsz, seq, D_MODEL), F32),
        scratch_shapes=[
            pltpu.VMEM((N_HEADS, HEAD_DIM, HEAD_DIM), F32),
            pltpu.VMEM((N_HEADS, HEAD_DIM, HEAD_DIM), F32),
            pltpu.VMEM((CONV_HALO, WIDTH), F32),
        ],
        compiler_params=_params(("parallel", "arbitrary")),
        name="token_mixer",
    )(x, mod, cos_t, sin_t,
      w_in, vec(ret_norm_g, WIDTH), dmask, qdec, kdec,
      w_in, lb_logits.astype(F32), vec(hgrn_norm_g, WIDTH), tri, lvl,
      w_in, conv_w, vec(conv_b, WIDTH), vec(conv_ln_g, WIDTH), vec(conv_ln_b, WIDTH),
      w_in, w_in, w_in, w_branch, w_out, vec(ln_g, D_MODEL), vec(ln_b, D_MODEL))


def _ffn_kernel(x_ref, mod_ref, wu_ref, cw_ref, cb_ref, wd_ref, lg_ref, lb_ref,
                o_ref, hext_ref, act_ref, *, alpha):
    tile = x_ref.shape[1]

    @pl.when(pl.program_id(1) == 0)
    def _():
        hext_ref[0:FFN_HALO, :] = jnp.zeros((FFN_HALO, D_MODEL), BF16)

    x = x_ref[0]
    hext_ref[FFN_HALO:FFN_HALO + tile, :] = _modulate(x, mod_ref, 3).astype(BF16)
    hext = hext_ref[...]
    first = FFN_HALO - (FFN_CONV_KERNEL - 1)

    def conv(p, col):
        y = cb_ref[:, col:col + FFN_CHUNK]
        for j in range(FFN_CONV_KERNEL):
            y = y + p[first + j:first + j + tile] * cw_ref[j:j + 1, col:col + FFN_CHUNK]
        return y

    for c in range(D_FF // FFN_CHUNK):
        col_a = c * FFN_CHUNK
        col_v = D_FF + col_a
        pa = _dot(hext, wu_ref[:, col_a:col_a + FFN_CHUNK])
        pv = _dot(hext, wu_ref[:, col_v:col_v + FFN_CHUNK])
        act_ref[:, col_a:col_a + FFN_CHUNK] = (_silu_half(conv(pa, col_a)) * conv(pv, col_v)).astype(BF16)

    hext_ref[0:FFN_HALO, :] = hext_ref[tile:tile + FFN_HALO, :]
    res_gate = mod_ref[0, 5:6, :]
    for r in range(FFN_OUT_BLOCKS):
        rows = slice(r * tile // FFN_OUT_BLOCKS, (r + 1) * tile // FFN_OUT_BLOCKS)
        y = _dot(act_ref[rows, :], wd_ref[...])
        o_ref[0, rows, :] = _layer_norm(alpha * x[rows] + res_gate * y, lg_ref[...], lb_ref[...])


def _conv_ffn(x, mod, w_up, conv_w, conv_b, w_down, ln_g, ln_b, alpha):
    bsz, seq, _ = x.shape
    tile = FFN_TILE
    row = lambda b, i: (b, i, 0)
    return pl.pallas_call(
        functools.partial(_ffn_kernel, alpha=alpha),
        grid=(bsz, seq // tile),
        in_specs=[
            pl.BlockSpec((1, tile, D_MODEL), row),
            pl.BlockSpec((1, N_MOD, D_MODEL), lambda b, i: (b, 0, 0)),
            _const_spec((D_MODEL, 2 * D_FF)),
            _const_spec((FFN_CONV_KERNEL, 2 * D_FF)),
            _const_spec((1, 2 * D_FF)),
            _const_spec((D_FF, D_MODEL)),
            _const_spec((1, D_MODEL)),
            _const_spec((1, D_MODEL)),
        ],
        out_specs=pl.BlockSpec((1, tile, D_MODEL), row),
        out_shape=jax.ShapeDtypeStruct((bsz, seq, D_MODEL), F32),
        scratch_shapes=[
            pltpu.VMEM((FFN_HALO + tile, D_MODEL), BF16),
            pltpu.VMEM((tile, D_FF), BF16),
        ],
        compiler_params=_params(("parallel", "arbitrary")),
        name="conv_ffn_ln",
    )(x, mod, w_up, conv_w, conv_b.reshape(1, 2 * D_FF), w_down,
      ln_g.reshape(1, D_MODEL), ln_b.reshape(1, D_MODEL))


def kernel(x, c, positions, w_ada, b_ada, w_in, ret_norm_g, hgrn_lb_logits, hgrn_norm_g, conv_w, conv_b,
           conv_ln_g, conv_ln_b, w_branch, w_out, ln1_g, ln1_b, ffn_w_up, ffn_conv_w, ffn_conv_b,
           ffn_w_down, ln2_g, ln2_b):
    depth = w_ada.shape[0]
    bsz = x.shape[0]
    alpha = (2.0 * depth) ** 0.25
    mod_all = _ada_mod(c, w_ada, b_ada)
    cos_t, sin_t = _rope_tables(positions)
    in_scale = np.ones((w_in.shape[-1],), np.float32)
    for a, b in ((3, 4), (4, 5), (7, 8), (8, 10), (10, 16)):
        in_scale[a * WIDTH:b * WIDTH] = 0.5
    w_in_b = (w_in * in_scale).astype(BF16)
    ffn_scale = np.concatenate([np.full((D_FF,), 0.5, np.float32), np.ones((D_FF,), np.float32)])
    for l in range(depth):
        mod = mod_all[l].reshape(bsz, N_MOD, D_MODEL)
        x = _token_mixer(x, mod, cos_t, sin_t, w_in_b, ret_norm_g[l], hgrn_lb_logits, hgrn_norm_g[l],
                         conv_w[l], conv_b[l], conv_ln_g[l], conv_ln_b[l],
                         (0.5 * w_branch[l]).astype(BF16), w_out[l].astype(BF16), ln1_g[l], ln1_b[l], l, alpha)
        x = _conv_ffn(x, mod, ffn_w_up[l].astype(BF16), ffn_conv_w[l] * ffn_scale, ffn_conv_b[l] * ffn_scale,
                      ffn_w_down[l].astype(BF16), ln2_g[l], ln2_b[l], alpha)
    return x
```
